```python
import jax, jax.numpy as jnp
from jax import lax
import numpy as np

D_MODEL = 1024
BATCH = 16
SEQ = 2048
DEPTH = 2

ATT_HEADS = 8
ATT_KV_HEADS = 2
ATT_HEAD_DIM = 64
ATT_GROUP = ATT_HEADS // ATT_KV_HEADS
WINDOW = 128
ATT_BLOCK = 128
RET_HEADS = 4
RET_KEY_DIM = 128
RET_VAL_DIM = 256
RET_CHUNK = 128
D_FF = -(-8 * D_MODEL // (3 * 256)) * 256
EPS = 1e-6

ATT_Q_W = ATT_HEADS * ATT_HEAD_DIM
ATT_KV_W = ATT_KV_HEADS * ATT_HEAD_DIM
RET_QK_W = RET_HEADS * RET_KEY_DIM
RET_V_W = RET_HEADS * RET_VAL_DIM
IN_WIDTHS = (ATT_Q_W, ATT_KV_W, ATT_KV_W, RET_QK_W, RET_QK_W, RET_V_W, RET_V_W, D_MODEL, D_MODEL)
D_IN = sum(IN_WIDTHS)

kernel_name = "hybrid_swa_sink_alibi_retention_gated_swiglu"


def rms_norm(x, g):
    xf = x.astype(jnp.float32)
    y = xf * lax.rsqrt(jnp.mean(xf * xf, axis=-1, keepdims=True) + EPS) * g.astype(jnp.float32)
    return y.astype(x.dtype)


def alibi_slopes(n_heads):
    return jnp.exp2(-8.0 * (jnp.arange(n_heads, dtype=jnp.float32) + 1.0) / n_heads)


def sliding_window_attention(q, k, v, sinks):
    b, s, _, hd = q.shape
    nb = s // ATT_BLOCK
    f32 = jnp.float32
    qb = q.astype(f32).reshape(b, nb, ATT_BLOCK, ATT_KV_HEADS, ATT_GROUP, hd) * (hd ** -0.5)

    def band(t):
        tp = jnp.pad(t.astype(f32), ((0, 0), (ATT_BLOCK, 0), (0, 0), (0, 0)))
        prev = tp[:, :s].reshape(b, nb, ATT_BLOCK, ATT_KV_HEADS, hd)
        cur = tp[:, ATT_BLOCK:].reshape(b, nb, ATT_BLOCK, ATT_KV_HEADS, hd)
        return jnp.concatenate([prev, cur], axis=2)

    kb, vb = band(k), band(v)
    i = jnp.arange(ATT_BLOCK)[:, None]
    j = jnp.arange(2 * ATT_BLOCK)[None, :]
    dist = i + ATT_BLOCK - j
    key_pos = jnp.arange(nb)[:, None, None] * ATT_BLOCK - ATT_BLOCK + j[None]
    valid = (dist >= 0) & (dist < WINDOW) & (key_pos >= 0)
    slopes = alibi_slopes(ATT_HEADS).reshape(ATT_KV_HEADS, ATT_GROUP)
    scores = jnp.einsum('bnikgd,bnjkd->bnkgij', qb, kb) - slopes[:, :, None, None] * dist.astype(f32)
    scores = jnp.where(valid[None, :, None, None], scores, -jnp.inf)
    sink = jnp.broadcast_to(sinks.astype(f32).reshape(ATT_KV_HEADS, ATT_GROUP)[None, None, :, :, None, None],
                            scores.shape[:-1] + (1,))
    probs = jax.nn.softmax(jnp.concatenate([scores, sink], axis=-1), axis=-1)[..., :-1]
    out = jnp.einsum('bnkgij,bnjkd->bnikgd', probs, vb)
    return out.reshape(b, s, ATT_Q_W)


def retention_chunkwise(q, k, v):
    b, s, h, dk = q.shape
    dv = v.shape[-1]
    nc = s // RET_CHUNK
    f32 = jnp.float32
    log_g = jnp.log(1.0 - jnp.exp2(-5.0 - jnp.arange(h, dtype=f32)))
    idx = jnp.arange(RET_CHUNK, dtype=f32)
    diff = idx[:, None] - idx[None, :]
    decay_intra = jnp.where(diff >= 0, jnp.exp(log_g[:, None, None] * jnp.maximum(diff, 0.0)), 0.0)
    decay_q = jnp.exp(log_g[:, None] * (idx + 1.0)).T[None, :, :, None]
    decay_k = jnp.exp(log_g[:, None] * (RET_CHUNK - 1.0 - idx)).T[None, :, :, None]
    decay_chunk = jnp.exp(log_g * RET_CHUNK)[None, :, None, None]

    def to_chunks(t):
        return t.astype(f32).reshape(b, nc, RET_CHUNK, h, t.shape[-1]).transpose(1, 0, 2, 3, 4)

    qc, kc, vc = to_chunks(q), to_chunks(k * (dk ** -0.5)), to_chunks(v)

    def step(state, inp):
        qi, ki, vi = inp
        scores = jnp.einsum('bihd,bjhd->bhij', qi, ki) * decay_intra
        o = jnp.einsum('bhij,bjhv->bihv', scores, vi)
        o = o + jnp.einsum('bihd,bhdv->bihv', qi, state) * decay_q
        state = decay_chunk * state + jnp.einsum('bjhd,bjhv->bhdv', ki * decay_k, vi)
        return state, o

    state0 = jnp.zeros((b, h, dk, dv), f32)
    _, o = lax.scan(step, state0, (qc, kc, vc))
    return o.transpose(1, 0, 2, 3, 4).reshape(b, s, h, dv)


def head_group_norm(o, gain):
    b, s, h, dv = o.shape
    mu = jnp.mean(o, axis=-1, keepdims=True)
    var = jnp.mean(jnp.square(o - mu), axis=-1, keepdims=True)
    return ((o - mu) * lax.rsqrt(var + EPS)).reshape(b, s, h * dv) * gain.astype(jnp.float32)


def setup_inputs(seed: int = 0) -> dict:
    key = jax.random.key(seed)
    ks = jax.random.split(key, 16)
    f32 = jnp.float32

    def w(k, shape, fan_in):
        return jax.random.normal(k, shape, f32) * (fan_in ** -0.5)

    return {
        "x": jax.random.normal(ks[0], (BATCH, SEQ, D_MODEL), f32),
        "norm_mix": 1.0 + 0.02 * jax.random.normal(ks[1], (DEPTH, D_MODEL), f32),
        "w_in": w(ks[2], (DEPTH, D_MODEL, D_IN), D_MODEL),
        "att_sinks": 0.5 * jax.random.normal(ks[3], (DEPTH, ATT_HEADS), f32),
        "ret_gn_gain": 1.0 + 0.02 * jax.random.normal(ks[4], (DEPTH, RET_V_W), f32),
        "w_att_o": w(ks[5], (DEPTH, ATT_Q_W, D_MODEL), ATT_Q_W),
        "w_ret_o": w(ks[6], (DEPTH, RET_V_W, D_MODEL), RET_V_W),
        "w_out": w(ks[7], (DEPTH, D_MODEL, D_MODEL), D_MODEL),
        "norm_ffn": 1.0 + 0.02 * jax.random.normal(ks[8], (DEPTH, D_MODEL), f32),
        "w_gate": w(ks[9], (DEPTH, D_MODEL, D_FF), D_MODEL),
        "w_up": w(ks[10], (DEPTH, D_MODEL, D_FF), D_MODEL),
        "w_down": w(ks[11], (DEPTH, D_FF, D_MODEL), D_FF),
        "final_norm": 1.0 + 0.02 * jax.random.normal(ks[12], (D_MODEL,), f32),
    }


def reference(x, norm_mix, w_in, att_sinks, ret_gn_gain, w_att_o, w_ret_o, w_out,
              norm_ffn, w_gate, w_up, w_down, final_norm):
    b, s, _ = x.shape
    offsets = [int(o) for o in np.cumsum(IN_WIDTHS)[:-1]]
    for l in range(DEPTH):
        h = rms_norm(x, norm_mix[l])
        proj = h @ w_in[l]
        aq, ak, av, rq, rk, rv, rg, ga, gr = jnp.split(proj, offsets, axis=-1)
        att = sliding_window_attention(
            aq.reshape(b, s, ATT_HEADS, ATT_HEAD_DIM),
            ak.reshape(b, s, ATT_KV_HEADS, ATT_HEAD_DIM),
            av.reshape(b, s, ATT_KV_HEADS, ATT_HEAD_DIM),
            att_sinks[l]).astype(x.dtype)
        ret = retention_chunkwise(
            rq.reshape(b, s, RET_HEADS, RET_KEY_DIM),
            rk.reshape(b, s, RET_HEADS, RET_KEY_DIM),
            rv.reshape(b, s, RET_HEADS, RET_VAL_DIM))
        ret = head_group_norm(ret, ret_gn_gain[l]).astype(x.dtype) * jax.nn.silu(rg)
        merged = jax.nn.sigmoid(ga) * (att @ w_att_o[l]) + jax.nn.sigmoid(gr) * (ret @ w_ret_o[l])
        x = x + merged @ w_out[l]
        h = rms_norm(x, norm_ffn[l])
        x = x + (jax.nn.silu(h @ w_gate[l]) * (h @ w_up[l])) @ w_down[l]
    return rms_norm(x, final_norm)
```

```python
import functools

import jax
import jax.numpy as jnp
from jax import lax
from jax.experimental import pallas as pl
from jax.experimental.pallas import tpu as pltpu

F32 = jnp.float32
BF16 = jnp.bfloat16

D_MODEL = 1024
ATT_HEADS = 8
ATT_KV_HEADS = 2
ATT_HEAD_DIM = 64
ATT_GROUP = ATT_HEADS // ATT_KV_HEADS
WINDOW = 128
CHUNK = 128
RET_HEADS = 4
RET_KEY_DIM = 128
RET_VAL_DIM = 256
D_FF = 2816
EPS = 1e-6

ATT_Q_W = ATT_HEADS * ATT_HEAD_DIM
ATT_KV_W = ATT_KV_HEADS * ATT_HEAD_DIM
RET_QK_W = RET_HEADS * RET_KEY_DIM
RET_V_W = RET_HEADS * RET_VAL_DIM
OFF_AQ = 0
OFF_AK = OFF_AQ + ATT_Q_W
OFF_AV = OFF_AK + ATT_KV_W
OFF_RQ = OFF_AV + ATT_KV_W
OFF_RK = OFF_RQ + RET_QK_W
OFF_RV = OFF_RK + RET_QK_W
OFF_RG = OFF_RV + RET_V_W
OFF_GA = OFF_RG + RET_V_W
OFF_GR = OFF_GA + D_MODEL
D_IN = OFF_GR + D_MODEL

NEG = -1e30

V7X_VMEM_BYTES = 64 * 1024 * 1024
VMEM_LIMIT = 56 * 1024 * 1024

PROJ_BM = 512
PROJ_NCH = 256
MIX_BLK = 512
FFN_BM = 512
FFN_CH = 256


def _rms(x, g):
    return x * lax.rsqrt(jnp.mean(x * x, axis=-1, keepdims=True) + EPS) * g


def _dot(a, b):
    return jnp.dot(a, b, preferred_element_type=F32)


def _dot_nt(a, b):
    return lax.dot_general(a, b, (((1,), (1,)), ((), ())), preferred_element_type=F32)


def _dot_tn(a, b):
    return lax.dot_general(a, b, (((0,), (0,)), ((), ())), preferred_element_type=F32)


def _inproj_kernel(x_ref, g_ref, w_ref, o_ref, kv_ref):
    h = _rms(x_ref[...], g_ref[...]).astype(BF16)
    for c in range(D_IN // PROJ_NCH):
        cols = slice(c * PROJ_NCH, (c + 1) * PROJ_NCH)
        y = _dot(h, w_ref[:, cols]).astype(BF16)
        o_ref[:, cols] = y
        if c * PROJ_NCH == OFF_AK:
            kv_ref[...] = y


def _inproj(x2, g, w):
    t = x2.shape[0]
    assert PROJ_NCH == 2 * ATT_KV_W and OFF_AK % PROJ_NCH == 0
    return pl.pallas_call(
        _inproj_kernel,
        grid=(t // PROJ_BM,),
        in_specs=[
            pl.BlockSpec((PROJ_BM, D_MODEL), lambda i: (i, 0)),
            pl.BlockSpec((1, D_MODEL), lambda i: (0, 0)),
            pl.BlockSpec((D_MODEL, D_IN), lambda i: (0, 0), pipeline_mode=pl.Buffered(1)),
        ],
        out_specs=[pl.BlockSpec((PROJ_BM, D_IN), lambda i: (i, 0)),
                   pl.BlockSpec((PROJ_BM, 2 * ATT_KV_W), lambda i: (i, 0))],
        out_shape=[jax.ShapeDtypeStruct((t, D_IN), BF16), jax.ShapeDtypeStruct((t, 2 * ATT_KV_W), BF16)],
        compiler_params=pltpu.CompilerParams(
            dimension_semantics=("arbitrary",), vmem_limit_bytes=VMEM_LIMIT),
        name="inproj",
    )(x2, g, w)


def _mixer_kernel(p_ref, kvp_ref, x_ref, bias_ref, fmask_ref, sink_ref, dintra_ref, dq_ref, dk_ref, dc_ref,
                  gain_ref, wa_ref, wr_ref, wo_ref, o_ref,
                  state, att_s, ret_s):
    j = pl.program_id(1)

    @pl.when(j == 0)
    def _():
        state[...] = jnp.zeros_like(state)

    first = fmask_ref[...] * jnp.where(j == 0, 1.0, 0.0)

    for c in range(MIX_BLK // CHUNK):
        rows = slice(c * CHUNK, (c + 1) * CHUNK)
        if c == 0:
            kv_p = kvp_ref[...]
        else:
            kv_p = p_ref[(c - 1) * CHUNK:c * CHUNK, OFF_AK:OFF_AK + 2 * ATT_KV_W]
        kv_c = p_ref[rows, OFF_AK:OFF_AK + 2 * ATT_KV_W]
        kv = jnp.concatenate([kv_p, kv_c], axis=0)
        q = p_ref[rows, OFF_AQ:OFF_AQ + ATT_Q_W]
        outs = []
        for kh in range(ATT_KV_HEADS):
            qs = jnp.concatenate(
                [q[:, (kh * ATT_GROUP + g) * ATT_HEAD_DIM:(kh * ATT_GROUP + g + 1) * ATT_HEAD_DIM]
                 for g in range(ATT_GROUP)], axis=0)
            kk = kv[:, kh * ATT_HEAD_DIM:(kh + 1) * ATT_HEAD_DIM]
            vv = kv[:, ATT_KV_W + kh * ATT_HEAD_DIM:ATT_KV_W + (kh + 1) * ATT_HEAD_DIM]
            s = _dot_nt(qs, kk) * (ATT_HEAD_DIM ** -0.5) + bias_ref[kh]
            if c == 0:
                s = s + first
            sink = sink_ref[kh]
            m = jnp.maximum(jnp.max(s, axis=-1, keepdims=True), sink)
            p = jnp.exp(s - m)
            denom = jnp.sum(p, axis=-1, keepdims=True) + jnp.exp(sink - m)
            o = _dot(p.astype(BF16), vv) / denom
            for g in range(ATT_GROUP):
                outs.append(o[g * CHUNK:(g + 1) * CHUNK])
        att_s[rows, :] = jnp.concatenate(outs, axis=1).astype(BF16)

        for h in range(RET_HEADS):
            rq = p_ref[rows, OFF_RQ + h * RET_KEY_DIM:OFF_RQ + (h + 1) * RET_KEY_DIM]
            rk = p_ref[rows, OFF_RK + h * RET_KEY_DIM:OFF_RK + (h + 1) * RET_KEY_DIM]
            rv = p_ref[rows, OFF_RV + h * RET_VAL_DIM:OFF_RV + (h + 1) * RET_VAL_DIM]
            st = state[h]
            sc = (_dot_nt(rq, rk) * dintra_ref[h]).astype(BF16)
            o = _dot(sc, rv) + _dot(rq, st.astype(BF16)) * dq_ref[h]
            kd = (rk.astype(F32) * dk_ref[h]).astype(BF16)
            state[h] = dc_ref[h] * st + _dot_tn(kd, rv)
            mu = jnp.mean(o, axis=-1, keepdims=True)
            oc = o - mu
            var = jnp.mean(oc * oc, axis=-1, keepdims=True)
            vcols = slice(h * RET_VAL_DIM, (h + 1) * RET_VAL_DIM)
            y = oc * lax.rsqrt(var + EPS) * gain_ref[:, vcols]
            rg = p_ref[rows, OFF_RG + h * RET_VAL_DIM:OFF_RG + (h + 1) * RET_VAL_DIM].astype(F32)
            ret_s[rows, vcols] = (y * (rg * jax.nn.sigmoid(rg))).astype(BF16)

    a = _dot(att_s[...], wa_ref[...])
    r = _dot(ret_s[...], wr_ref[...])
    ga = p_ref[:, OFF_GA:OFF_GA + D_MODEL].astype(F32)
    gr = p_ref[:, OFF_GR:OFF_GR + D_MODEL].astype(F32)
    merged = (jax.nn.sigmoid(ga) * a + jax.nn.sigmoid(gr) * r).astype(BF16)
    o_ref[...] = x_ref[...] + _dot(merged, wo_ref[...])


def _mixer(proj, kv, x2, consts, sink_col, gain, wa, wr, wo, batch, seq):
    bias, fmask, dintra, dq, dk, dc = consts
    nblk = seq // MIX_BLK
    cpb = MIX_BLK // CHUNK
    row = lambda b, j: (b * nblk + j, 0)
    halo = lambda b, j: (jnp.maximum((b * nblk + j) * cpb - 1, 0), 0)
    const2 = lambda b, j: (0, 0)
    const3 = lambda b, j: (0, 0, 0)
    resident = functools.partial(pl.BlockSpec, pipeline_mode=pl.Buffered(1))
    return pl.pallas_call(
        _mixer_kernel,
        grid=(batch, nblk),
        in_specs=[
            pl.BlockSpec((MIX_BLK, D_IN), row),
            pl.BlockSpec((CHUNK, 2 * ATT_KV_W), halo),
            pl.BlockSpec((MIX_BLK, D_MODEL), row),
            resident(bias.shape, const3),
            resident(fmask.shape, const2),
            resident(sink_col.shape, const3),
            resident(dintra.shape, const3),
            resident(dq.shape, const3),
            resident(dk.shape, const3),
            pl.BlockSpec(memory_space=pltpu.SMEM),
            resident(gain.shape, const2),
            resident(wa.shape, const2),
            resident(wr.shape, const2),
            resident(wo.shape, const2),
        ],
        out_specs=pl.BlockSpec((MIX_BLK, D_MODEL), row),
        out_shape=jax.ShapeDtypeStruct(x2.shape, F32),
        scratch_shapes=[
            pltpu.VMEM((RET_HEADS, RET_KEY_DIM, RET_VAL_DIM), F32),
            pltpu.VMEM((MIX_BLK, ATT_Q_W), BF16),
            pltpu.VMEM((MIX_BLK, RET_V_W), BF16),
        ],
        compiler_params=pltpu.CompilerParams(
            dimension_semantics=("arbitrary", "arbitrary"), vmem_limit_bytes=VMEM_LIMIT),
        name="mixer",
    )(proj, kv, x2, bias, fmask, sink_col, dintra, dq, dk, dc, gain, wa, wr, wo)


def _mixer_constants():
    i = jnp.arange(CHUNK)[:, None]
    jj = jnp.arange(2 * CHUNK)[None, :]
    dist = i + CHUNK - jj
    valid = (dist >= 0) & (dist < WINDOW)
    slopes = jnp.exp2(-8.0 * (jnp.arange(ATT_HEADS, dtype=F32) + 1.0) / ATT_HEADS)
    bias = jnp.where(valid[None], -slopes[:, None, None] * dist.astype(F32)[None], NEG)
    bias = bias.reshape(ATT_KV_HEADS, ATT_GROUP * CHUNK, 2 * CHUNK)
    fmask = jnp.where(jj < CHUNK, NEG, 0.0).astype(F32)

    log_g = jnp.log(1.0 - jnp.exp2(-5.0 - jnp.arange(RET_HEADS, dtype=F32)))
    idx = jnp.arange(CHUNK, dtype=F32)
    diff = idx[:, None] - idx[None, :]
    scale = RET_KEY_DIM ** -0.5
    dintra = jnp.where(diff >= 0, jnp.exp(log_g[:, None, None] * jnp.maximum(diff, 0.0)), 0.0) * scale
    dq = jnp.exp(log_g[:, None] * (idx + 1.0))[:, :, None]
    dk = (jnp.exp(log_g[:, None] * (CHUNK - 1.0 - idx)) * scale)[:, :, None]
    dc = jnp.exp(log_g * CHUNK)
    return bias, fmask, dintra, dq, dk, dc


def _ffn_kernel(x_ref, g_ref, wg_ref, wu_ref, wd_ref, fg_ref, o_ref, *, final):
    x = x_ref[...]
    h = _rms(x, g_ref[...]).astype(BF16)
    acc = x
    for c in range(D_FF // FFN_CH):
        cols = slice(c * FFN_CH, (c + 1) * FFN_CH)
        gate = _dot(h, wg_ref[:, cols])
        up = _dot(h, wu_ref[:, cols])
        act = (gate * jax.nn.sigmoid(gate) * up).astype(BF16)
        acc = acc + _dot(act, wd_ref[cols, :])
    if final:
        acc = _rms(acc, fg_ref[...])
    o_ref[...] = acc


def _ffn(x2, g, wg, wu, wd, fg, final):
    t = x2.shape[0]
    const = lambda i: (0, 0)
    resident = functools.partial(pl.BlockSpec, pipeline_mode=pl.Buffered(1))
    return pl.pallas_call(
        functools.partial(_ffn_kernel, final=final),
        grid=(t // FFN_BM,),
        in_specs=[
            pl.BlockSpec((FFN_BM, D_MODEL), lambda i: (i, 0)),
            pl.BlockSpec((1, D_MODEL), const),
            resident(wg.shape, const),
            resident(wu.shape, const),
            resident(wd.shape, const),
            pl.BlockSpec((1, D_MODEL), const),
        ],
        out_specs=pl.BlockSpec((FFN_BM, D_MODEL), lambda i: (i, 0)),
        out_shape=jax.ShapeDtypeStruct(x2.shape, F32),
        compiler_params=pltpu.CompilerParams(
            dimension_semantics=("arbitrary",), vmem_limit_bytes=VMEM_LIMIT),
        name="ffn_final" if final else "ffn",
    )(x2, g, wg, wu, wd, fg)


def kernel(x, norm_mix, w_in, att_sinks, ret_gn_gain, w_att_o, w_ret_o, w_out, norm_ffn, w_gate, w_up, w_down,
           final_norm):
    batch, seq, d = x.shape
    depth = w_in.shape[0]
    assert d == D_MODEL and seq % MIX_BLK == 0 and (batch * seq) % PROJ_BM == 0 and (batch * seq) % FFN_BM == 0
    x2 = x.reshape(batch * seq, d)
    consts = _mixer_constants()
    fg = final_norm.reshape(1, d)
    for l in range(depth):
        proj, kv = _inproj(x2, norm_mix[l].reshape(1, d), w_in[l].astype(BF16))
        sink_col = jnp.repeat(att_sinks[l].reshape(ATT_KV_HEADS, ATT_GROUP), CHUNK, axis=1)[:, :, None]
        x2 = _mixer(proj, kv, x2, consts, sink_col.astype(F32), ret_gn_gain[l].reshape(1, RET_V_W),
                    w_att_o[l].astype(BF16), w_ret_o[l].astype(BF16), w_out[l].astype(BF16), batch, seq)
        x2 = _ffn(x2, norm_ffn[l].reshape(1, d), w_gate[l].astype(BF16), w_up[l].astype(BF16),
                  w_down[l].astype(BF16), fg, final=(l == depth - 1))
    return x2.reshape(batch, seq, d)
```

```python
import functools

import jax
import jax.numpy as jnp
from jax import lax
from jax.experimental import pallas as pl
from jax.experimental.pallas import tpu as pltpu

F32 = jnp.float32
BF16 = jnp.bfloat16

D_MODEL = 1024
ATT_HEADS = 8
ATT_KV_HEADS = 2
ATT_HEAD_DIM = 64
ATT_GROUP = ATT_HEADS // ATT_KV_HEADS
WINDOW = 128
CHUNK = 128
RET_HEADS = 4
RET_KEY_DIM = 128
RET_VAL_DIM = 256
D_FF = 2816
EPS = 1e-6

ATT_Q_W = ATT_HEADS * ATT_HEAD_DIM
ATT_KV_W = ATT_KV_HEADS * ATT_HEAD_DIM
RET_QK_W = RET_HEADS * RET_KEY_DIM
RET_V_W = RET_HEADS * RET_VAL_DIM
OFF_AQ = 0
OFF_AK = OFF_AQ + ATT_Q_W
OFF_AV = OFF_AK + ATT_KV_W
OFF_RQ = OFF_AV + ATT_KV_W
OFF_RK = OFF_RQ + RET_QK_W
OFF_RV = OFF_RK + RET_QK_W
OFF_RG = OFF_RV + RET_V_W
OFF_GA = OFF_RG + RET_V_W
OFF_GR = OFF_GA + D_MODEL
D_IN = OFF_GR + D_MODEL

NEG = -1e30

V7X_VMEM_BYTES = 64 * 1024 * 1024
VMEM_LIMIT = 56 * 1024 * 1024

PROJ_BM = 512
PROJ_NCH = 256
MIX_BLK = 512
FFN_BM = 512
FFN_CH = 256


def _rms(x, g):
    return x * lax.rsqrt(jnp.mean(x * x, axis=-1, keepdims=True) + EPS) * g


def _dot(a, b):
    return jnp.dot(a, b, preferred_element_type=F32)


def _dot_nt(a, b):
    return lax.dot_general(a, b, (((1,), (1,)), ((), ())), preferred_element_type=F32)


def _dot_tn(a, b):
    return lax.dot_general(a, b, (((0,), (0,)), ((), ())), preferred_element_type=F32)


def _inproj_kernel(x_ref, g_ref, w_ref, o_ref, kv_ref):
    h = _rms(x_ref[...], g_ref[...]).astype(BF16)
    for c in range(D_IN // PROJ_NCH):
        cols = slice(c * PROJ_NCH, (c + 1) * PROJ_NCH)
        y = _dot(h, w_ref[:, cols]).astype(BF16)
        o_ref[:, cols] = y
        if c * PROJ_NCH == OFF_AK:
            kv_ref[...] = y


def _inproj(x2, g, w):
    t = x2.shape[0]
    assert PROJ_NCH == 2 * ATT_KV_W and OFF_AK % PROJ_NCH == 0
    return pl.pallas_call(
        _inproj_kernel,
        grid=(t // PROJ_BM,),
        in_specs=[
            pl.BlockSpec((PROJ_BM, D_MODEL), lambda i: (i, 0)),
            pl.BlockSpec((1, D_MODEL), lambda i: (0, 0)),
            pl.BlockSpec((D_MODEL, D_IN), lambda i: (0, 0), pipeline_mode=pl.Buffered(1)),
        ],
        out_specs=[pl.BlockSpec((PROJ_BM, D_IN), lambda i: (i, 0)),
                   pl.BlockSpec((PROJ_BM, 2 * ATT_KV_W), lambda i: (i, 0))],
        out_shape=[jax.ShapeDtypeStruct((t, D_IN), BF16), jax.ShapeDtypeStruct((t, 2 * ATT_KV_W), BF16)],
        compiler_params=pltpu.CompilerParams(
            dimension_semantics=("arbitrary",), vmem_limit_bytes=VMEM_LIMIT),
        name="inproj",
    )(x2, g, w)


def _mixer_kernel(p_ref, kvp_ref, x_ref, bias_ref, sink_ref, dintra_ref, dq_ref, dk_ref, dc_ref,
                  gain_ref, wa_ref, wr_ref, wo_ref, o_ref,
                  state, att_t, ret_s):
    j = pl.program_id(1)

    @pl.when(j == 0)
    def _():
        state[...] = jnp.zeros_like(state)

    first = jnp.where(j == 0, 1, 0)

    for c in range(MIX_BLK // CHUNK):
        rows = slice(c * CHUNK, (c + 1) * CHUNK)
        if c == 0:
            kv_p = kvp_ref[...]
        else:
            kv_p = p_ref[(c - 1) * CHUNK:c * CHUNK, OFF_AK:OFF_AK + 2 * ATT_KV_W]
        kv_c = p_ref[rows, OFF_AK:OFF_AK + 2 * ATT_KV_W]
        kv = jnp.concatenate([kv_p, kv_c], axis=0)
        q = p_ref[rows, OFF_AQ:OFF_AQ + ATT_Q_W]
        for kh in range(ATT_KV_HEADS):
            qs = jnp.concatenate(
                [q[:, (kh * ATT_GROUP + g) * ATT_HEAD_DIM:(kh * ATT_GROUP + g + 1) * ATT_HEAD_DIM]
                 for g in range(ATT_GROUP)], axis=0)
            kk = kv[:, kh * ATT_HEAD_DIM:(kh + 1) * ATT_HEAD_DIM] * (ATT_HEAD_DIM ** -0.5)
            vv = kv[:, ATT_KV_W + kh * ATT_HEAD_DIM:ATT_KV_W + (kh + 1) * ATT_HEAD_DIM]
            bias = bias_ref[first * ATT_KV_HEADS + kh] if c == 0 else bias_ref[kh]
            s = _dot_nt(kk, qs) + bias
            sink = sink_ref[kh]
            m = jnp.maximum(jnp.max(s, axis=0, keepdims=True), sink)
            p = jnp.exp(s - m)
            denom = jnp.sum(p, axis=0, keepdims=True) + jnp.exp(sink - m)
            o = _dot_tn(vv, p.astype(BF16)) / denom
            for g in range(ATT_GROUP):
                h = kh * ATT_GROUP + g
                att_t[h * ATT_HEAD_DIM:(h + 1) * ATT_HEAD_DIM, rows] = o[:, g * CHUNK:(g + 1) * CHUNK].astype(BF16)

        for h in range(RET_HEADS):
            rq = p_ref[rows, OFF_RQ + h * RET_KEY_DIM:OFF_RQ + (h + 1) * RET_KEY_DIM]
            rk = p_ref[rows, OFF_RK + h * RET_KEY_DIM:OFF_RK + (h + 1) * RET_KEY_DIM]
            rv = p_ref[rows, OFF_RV + h * RET_VAL_DIM:OFF_RV + (h + 1) * RET_VAL_DIM]
            st = state[h]
            sc = (_dot_nt(rq, rk) * dintra_ref[h]).astype(BF16)
            o = _dot(sc, rv) + _dot(rq, st.astype(BF16)) * dq_ref[h]
            kd = (rk.astype(F32) * dk_ref[h]).astype(BF16)
            state[h] = dc_ref[h] * st + _dot_tn(kd, rv)
            mu = jnp.mean(o, axis=-1, keepdims=True)
            oc = o - mu
            var = jnp.mean(oc * oc, axis=-1, keepdims=True)
            vcols = slice(h * RET_VAL_DIM, (h + 1) * RET_VAL_DIM)
            y = oc * lax.rsqrt(var + EPS) * gain_ref[:, vcols]
            rg = p_ref[rows, OFF_RG + h * RET_VAL_DIM:OFF_RG + (h + 1) * RET_VAL_DIM].astype(F32)
            ret_s[rows, vcols] = (y * (rg * jax.nn.sigmoid(rg))).astype(BF16)

    a = _dot_tn(att_t[...], wa_ref[...])
    r = _dot(ret_s[...], wr_ref[...])
    ga = p_ref[:, OFF_GA:OFF_GA + D_MODEL].astype(F32)
    gr = p_ref[:, OFF_GR:OFF_GR + D_MODEL].astype(F32)
    merged = (jax.nn.sigmoid(ga) * a + jax.nn.sigmoid(gr) * r).astype(BF16)
    o_ref[...] = x_ref[...] + _dot(merged, wo_ref[...])


def _mixer(proj, kv, x2, consts, sink_row, gain, wa, wr, wo, batch, seq):
    bias, dintra, dq, dk, dc = consts
    nblk = seq // MIX_BLK
    cpb = MIX_BLK // CHUNK
    row = lambda b, j: (b * nblk + j, 0)
    halo = lambda b, j: (jnp.maximum((b * nblk + j) * cpb - 1, 0), 0)
    const2 = lambda b, j: (0, 0)
    const3 = lambda b, j: (0, 0, 0)
    resident = functools.partial(pl.BlockSpec, pipeline_mode=pl.Buffered(1))
    return pl.pallas_call(
        _mixer_kernel,
        grid=(batch, nblk),
        in_specs=[
            pl.BlockSpec((MIX_BLK, D_IN), row),
            pl.BlockSpec((CHUNK, 2 * ATT_KV_W), halo),
            pl.BlockSpec((MIX_BLK, D_MODEL), row),
            resident(bias.shape, const3),
            resident(sink_row.shape, const3),
            resident(dintra.shape, const3),
            resident(dq.shape, const3),
            resident(dk.shape, const3),
            pl.BlockSpec(memory_space=pltpu.SMEM),
            resident(gain.shape, const2),
            resident(wa.shape, const2),
            resident(wr.shape, const2),
            resident(wo.shape, const2),
        ],
        out_specs=pl.BlockSpec((MIX_BLK, D_MODEL), row),
        out_shape=jax.ShapeDtypeStruct(x2.shape, F32),
        scratch_shapes=[
            pltpu.VMEM((RET_HEADS, RET_KEY_DIM, RET_VAL_DIM), F32),
            pltpu.VMEM((ATT_Q_W, MIX_BLK), BF16),
            pltpu.VMEM((MIX_BLK, RET_V_W), BF16),
        ],
        compiler_params=pltpu.CompilerParams(
            dimension_semantics=("arbitrary", "arbitrary"), vmem_limit_bytes=VMEM_LIMIT),
        name="mixer",
    )(proj, kv, x2, bias, sink_row, dintra, dq, dk, dc, gain, wa, wr, wo)


def _mixer_constants():
    i = jnp.arange(CHUNK)[None, :]
    jj = jnp.arange(2 * CHUNK)[:, None]
    dist = i + CHUNK - jj
    valid = (dist >= 0) & (dist < WINDOW)
    slopes = jnp.exp2(-8.0 * (jnp.arange(ATT_HEADS, dtype=F32) + 1.0) / ATT_HEADS)

    def table(ok):
        b = jnp.where(ok[None], -slopes[:, None, None] * dist.astype(F32)[None], NEG)
        b = b.reshape(ATT_KV_HEADS, ATT_GROUP, 2 * CHUNK, CHUNK).transpose(0, 2, 1, 3)
        return b.reshape(ATT_KV_HEADS, 2 * CHUNK, ATT_GROUP * CHUNK)

    bias = jnp.concatenate([table(valid), table(valid & (jj >= CHUNK))], axis=0)

    log_g = jnp.log(1.0 - jnp.exp2(-5.0 - jnp.arange(RET_HEADS, dtype=F32)))
    idx = jnp.arange(CHUNK, dtype=F32)
    diff = idx[:, None] - idx[None, :]
    scale = RET_KEY_DIM ** -0.5
    dintra = jnp.where(diff >= 0, jnp.exp(log_g[:, None, None] * jnp.maximum(diff, 0.0)), 0.0) * scale
    dq = jnp.exp(log_g[:, None] * (idx + 1.0))
    dk = jnp.exp(log_g[:, None] * (CHUNK - 1.0 - idx)) * scale
    dq = jnp.broadcast_to(dq[:, :, None], (RET_HEADS, CHUNK, RET_VAL_DIM))
    dk = jnp.broadcast_to(dk[:, :, None], (RET_HEADS, CHUNK, RET_KEY_DIM))
    dc = jnp.exp(log_g * CHUNK)
    return bias, dintra, dq, dk, dc


def _ffn_kernel(x_ref, g_ref, wg_ref, wu_ref, wd_ref, fg_ref, o_ref, *, final):
    x = x_ref[...]
    h = _rms(x, g_ref[...]).astype(BF16)
    acc = x
    for c in range(D_FF // FFN_CH):
        cols = slice(c * FFN_CH, (c + 1) * FFN_CH)
        gate = _dot(h, wg_ref[:, cols])
        up = _dot(h, wu_ref[:, cols])
        act = (gate * jax.nn.sigmoid(gate) * up).astype(BF16)
        acc = acc + _dot(act, wd_ref[cols, :])
    if final:
        acc = _rms(acc, fg_ref[...])
    o_ref[...] = acc


def _ffn(x2, g, wg, wu, wd, fg, final):
    t = x2.shape[0]
    const = lambda i: (0, 0)
    resident = functools.partial(pl.BlockSpec, pipeline_mode=pl.Buffered(1))
    return pl.pallas_call(
        functools.partial(_ffn_kernel, final=final),
        grid=(t // FFN_BM,),
        in_specs=[
            pl.BlockSpec((FFN_BM, D_MODEL), lambda i: (i, 0)),
            pl.BlockSpec((1, D_MODEL), const),
            resident(wg.shape, const),
            resident(wu.shape, const),
            resident(wd.shape, const),
            pl.BlockSpec((1, D_MODEL), const),
        ],
        out_specs=pl.BlockSpec((FFN_BM, D_MODEL), lambda i: (i, 0)),
        out_shape=jax.ShapeDtypeStruct(x2.shape, F32),
        compiler_params=pltpu.CompilerParams(
            dimension_semantics=("arbitrary",), vmem_limit_bytes=VMEM_LIMIT),
        name="ffn_final" if final else "ffn",
    )(x2, g, wg, wu, wd, fg)


def kernel(x, norm_mix, w_in, att_sinks, ret_gn_gain, w_att_o, w_ret_o, w_out, norm_ffn, w_gate, w_up, w_down,
           final_norm):
    batch, seq, d = x.shape
    depth = w_in.shape[0]
    assert d == D_MODEL and seq % MIX_BLK == 0 and (batch * seq) % PROJ_BM == 0 and (batch * seq) % FFN_BM == 0
    x2 = x.reshape(batch * seq, d)
    consts = _mixer_constants()
    fg = final_norm.reshape(1, d)
    for l in range(depth):
        proj, kv = _inproj(x2, norm_mix[l].reshape(1, d), w_in[l].astype(BF16))
        sink_row = jnp.repeat(att_sinks[l].astype(F32).reshape(ATT_KV_HEADS, ATT_GROUP), CHUNK, axis=1)[:, None, :]
        x2 = _mixer(proj, kv, x2, consts, sink_row, ret_gn_gain[l].reshape(1, RET_V_W),
                    w_att_o[l].astype(BF16), w_ret_o[l].astype(BF16), w_out[l].astype(BF16), batch, seq)
        x2 = _ffn(x2, norm_ffn[l].reshape(1, d), w_gate[l].astype(BF16), w_up[l].astype(BF16),
                  w_down[l].astype(BF16), fg, final=(l == depth - 1))
    return x2.reshape(batch, seq, d)
```

```python
import functools

import jax
import jax.numpy as jnp
from jax import lax
from jax.experimental import pallas as pl
from jax.experimental.pallas import tpu as pltpu

F32 = jnp.float32
BF16 = jnp.bfloat16

D_MODEL = 1024
ATT_HEADS = 8
ATT_KV_HEADS = 2
ATT_HEAD_DIM = 64
ATT_GROUP = ATT_HEADS // ATT_KV_HEADS
WINDOW = 128
CHUNK = 128
RET_HEADS = 4
RET_KEY_DIM = 128
RET_VAL_DIM = 256
D_FF = 2816
EPS = 1e-6

ATT_Q_W = ATT_HEADS * ATT_HEAD_DIM
ATT_KV_W = ATT_KV_HEADS * ATT_HEAD_DIM
RET_QK_W = RET_HEADS * RET_KEY_DIM
RET_V_W = RET_HEADS * RET_VAL_DIM
OFF_AQ = 0
OFF_AK = OFF_AQ + ATT_Q_W
OFF_AV = OFF_AK + ATT_KV_W
OFF_RQ = OFF_AV + ATT_KV_W
OFF_RK = OFF_RQ + RET_QK_W
OFF_RV = OFF_RK + RET_QK_W
OFF_RG = OFF_RV + RET_V_W
OFF_GA = OFF_RG + RET_V_W
OFF_GR = OFF_GA + D_MODEL
D_IN = OFF_GR + D_MODEL

NEG = -1e30

V7X_VMEM_BYTES = 64 * 1024 * 1024
VMEM_LIMIT = 56 * 1024 * 1024

PROJ_BM = 512
PROJ_NCH = 256
MIX_BLK = 512
FFN_BM = 512
FFN_CH = 256


def _rms(x, g):
    return x * lax.rsqrt(jnp.mean(x * x, axis=-1, keepdims=True) + EPS) * g


def _dot(a, b):
    return jnp.dot(a, b, preferred_element_type=F32)


def _dot_nt(a, b):
    return lax.dot_general(a, b, (((1,), (1,)), ((), ())), preferred_element_type=F32)


def _dot_tn(a, b):
    return lax.dot_general(a, b, (((0,), (0,)), ((), ())), preferred_element_type=F32)


def _inproj_kernel(x_ref, g_ref, w_ref, o_ref, kv_ref):
    h = _rms(x_ref[...], g_ref[...]).astype(BF16)
    for c in range(D_IN // PROJ_NCH):
        cols = slice(c * PROJ_NCH, (c + 1) * PROJ_NCH)
        y = _dot(h, w_ref[:, cols].astype(BF16)).astype(BF16)
        o_ref[:, cols] = y
        if c * PROJ_NCH == OFF_AK:
            kv_ref[...] = y


def _inproj(x2, g, w, layer):
    t = x2.shape[0]
    assert PROJ_NCH == 2 * ATT_KV_W and OFF_AK % PROJ_NCH == 0
    return pl.pallas_call(
        _inproj_kernel,
        grid=(t // PROJ_BM,),
        in_specs=[
            pl.BlockSpec((PROJ_BM, D_MODEL), lambda i: (i, 0)),
            pl.BlockSpec((1, D_MODEL), lambda i: (0, 0)),
            pl.BlockSpec((None, D_MODEL, D_IN), lambda i: (layer, 0, 0), pipeline_mode=pl.Buffered(1)),
        ],
        out_specs=[pl.BlockSpec((PROJ_BM, D_IN), lambda i: (i, 0)),
                   pl.BlockSpec((PROJ_BM, 2 * ATT_KV_W), lambda i: (i, 0))],
        out_shape=[jax.ShapeDtypeStruct((t, D_IN), BF16), jax.ShapeDtypeStruct((t, 2 * ATT_KV_W), BF16)],
        compiler_params=pltpu.CompilerParams(
            dimension_semantics=("arbitrary",), vmem_limit_bytes=VMEM_LIMIT),
        name="inproj",
    )(x2, g, w)


def _mixer_kernel(p_ref, kvp_ref, x_ref, bias_ref, sink_ref, dintra_ref, dq_ref, dk_ref, dc_ref,
                  gain_ref, wa_ref, wr_ref, wo_ref, o_ref,
                  state, att_t, ret_s):
    j = pl.program_id(1)

    @pl.when(j == 0)
    def _():
        state[...] = jnp.zeros_like(state)

    first = jnp.where(j == 0, 1, 0)

    for c in range(MIX_BLK // CHUNK):
        rows = slice(c * CHUNK, (c + 1) * CHUNK)
        if c == 0:
            kv_p = kvp_ref[...]
        else:
            kv_p = p_ref[(c - 1) * CHUNK:c * CHUNK, OFF_AK:OFF_AK + 2 * ATT_KV_W]
        kv_c = p_ref[rows, OFF_AK:OFF_AK + 2 * ATT_KV_W]
        kv = jnp.concatenate([kv_p, kv_c], axis=0)
        q = p_ref[rows, OFF_AQ:OFF_AQ + ATT_Q_W]
        for kh in range(ATT_KV_HEADS):
            qs = jnp.concatenate(
                [q[:, (kh * ATT_GROUP + g) * ATT_HEAD_DIM:(kh * ATT_GROUP + g + 1) * ATT_HEAD_DIM]
                 for g in range(ATT_GROUP)], axis=0)
            kk = kv[:, kh * ATT_HEAD_DIM:(kh + 1) * ATT_HEAD_DIM] * (ATT_HEAD_DIM ** -0.5)
            vv = kv[:, ATT_KV_W + kh * ATT_HEAD_DIM:ATT_KV_W + (kh + 1) * ATT_HEAD_DIM]
            bias = bias_ref[first * ATT_KV_HEADS + kh] if c == 0 else bias_ref[kh]
            s = _dot_nt(kk, qs) + bias
            sink = sink_ref[kh]
            m = jnp.maximum(jnp.max(s, axis=0, keepdims=True), sink)
            p = jnp.exp(s - m)
            denom = jnp.sum(p, axis=0, keepdims=True) + jnp.exp(sink - m)
            o = _dot_tn(vv, p.astype(BF16)) / denom
            for g in range(ATT_GROUP):
                h = kh * ATT_GROUP + g
                att_t[h * ATT_HEAD_DIM:(h + 1) * ATT_HEAD_DIM, rows] = o[:, g * CHUNK:(g + 1) * CHUNK].astype(BF16)

        for h in range(RET_HEADS):
            rq = p_ref[rows, OFF_RQ + h * RET_KEY_DIM:OFF_RQ + (h + 1) * RET_KEY_DIM]
            rk = p_ref[rows, OFF_RK + h * RET_KEY_DIM:OFF_RK + (h + 1) * RET_KEY_DIM]
            rv = p_ref[rows, OFF_RV + h * RET_VAL_DIM:OFF_RV + (h + 1) * RET_VAL_DIM]
            st = state[h]
            sc = (_dot_nt(rq, rk) * dintra_ref[h]).astype(BF16)
            o = _dot(sc, rv) + _dot(rq, st.astype(BF16)) * dq_ref[h]
            kd = (rk.astype(F32) * dk_ref[h]).astype(BF16)
            state[h] = dc_ref[h] * st + _dot_tn(kd, rv)
            mu = jnp.mean(o, axis=-1, keepdims=True)
            oc = o - mu
            var = jnp.mean(oc * oc, axis=-1, keepdims=True)
            vcols = slice(h * RET_VAL_DIM, (h + 1) * RET_VAL_DIM)
            y = oc * lax.rsqrt(var + EPS) * gain_ref[:, vcols]
            rg = p_ref[rows, OFF_RG + h * RET_VAL_DIM:OFF_RG + (h + 1) * RET_VAL_DIM].astype(F32)
            ret_s[rows, vcols] = (y * (rg * jax.nn.sigmoid(rg))).astype(BF16)

    a = _dot_tn(att_t[...], wa_ref[...])
    r = _dot(ret_s[...], wr_ref[...])
    ga = p_ref[:, OFF_GA:OFF_GA + D_MODEL].astype(F32)
    gr = p_ref[:, OFF_GR:OFF_GR + D_MODEL].astype(F32)
    merged = (jax.nn.sigmoid(ga) * a + jax.nn.sigmoid(gr) * r).astype(BF16)
    o_ref[...] = x_ref[...] + _dot(merged, wo_ref[...])


def _mixer(proj, kv, x2, consts, sink_row, gain, wa, wr, wo, batch, seq):
    bias, dintra, dq, dk, dc = consts
    nblk = seq // MIX_BLK
    cpb = MIX_BLK // CHUNK
    row = lambda b, j: (b * nblk + j, 0)
    halo = lambda b, j: (jnp.maximum((b * nblk + j) * cpb - 1, 0), 0)
    const2 = lambda b, j: (0, 0)
    const3 = lambda b, j: (0, 0, 0)
    resident = functools.partial(pl.BlockSpec, pipeline_mode=pl.Buffered(1))
    return pl.pallas_call(
        _mixer_kernel,
        grid=(batch, nblk),
        in_specs=[
            pl.BlockSpec((MIX_BLK, D_IN), row),
            pl.BlockSpec((CHUNK, 2 * ATT_KV_W), halo),
            pl.BlockSpec((MIX_BLK, D_MODEL), row),
            resident(bias.shape, const3),
            resident(sink_row.shape, const3),
            resident(dintra.shape, const3),
            resident(dq.shape, const3),
            resident(dk.shape, const3),
            pl.BlockSpec(memory_space=pltpu.SMEM),
            resident(gain.shape, const2),
            resident(wa.shape, const2),
            resident(wr.shape, const2),
            resident(wo.shape, const2),
        ],
        out_specs=pl.BlockSpec((MIX_BLK, D_MODEL), row),
        out_shape=jax.ShapeDtypeStruct(x2.shape, F32),
        scratch_shapes=[
            pltpu.VMEM((RET_HEADS, RET_KEY_DIM, RET_VAL_DIM), F32),
            pltpu.VMEM((ATT_Q_W, MIX_BLK), BF16),
            pltpu.VMEM((MIX_BLK, RET_V_W), BF16),
        ],
        compiler_params=pltpu.CompilerParams(
            dimension_semantics=("arbitrary", "arbitrary"), vmem_limit_bytes=VMEM_LIMIT),
        name="mixer",
    )(proj, kv, x2, bias, sink_row, dintra, dq, dk, dc, gain, wa, wr, wo)


def _mixer_constants():
    i = jnp.arange(CHUNK)[None, :]
    jj = jnp.arange(2 * CHUNK)[:, None]
    dist = i + CHUNK - jj
    valid = (dist >= 0) & (dist < WINDOW)
    slopes = jnp.exp2(-8.0 * (jnp.arange(ATT_HEADS, dtype=F32) + 1.0) / ATT_HEADS)

    def table(ok):
        b = jnp.where(ok[None], -slopes[:, None, None] * dist.astype(F32)[None], NEG)
        b = b.reshape(ATT_KV_HEADS, ATT_GROUP, 2 * CHUNK, CHUNK).transpose(0, 2, 1, 3)
        return b.reshape(ATT_KV_HEADS, 2 * CHUNK, ATT_GROUP * CHUNK)

    bias = jnp.concatenate([table(valid), table(valid & (jj >= CHUNK))], axis=0)

    log_g = jnp.log(1.0 - jnp.exp2(-5.0 - jnp.arange(RET_HEADS, dtype=F32)))
    idx = jnp.arange(CHUNK, dtype=F32)
    diff = idx[:, None] - idx[None, :]
    scale = RET_KEY_DIM ** -0.5
    dintra = jnp.where(diff >= 0, jnp.exp(log_g[:, None, None] * jnp.maximum(diff, 0.0)), 0.0) * scale
    dq = jnp.exp(log_g[:, None] * (idx + 1.0))
    dk = jnp.exp(log_g[:, None] * (CHUNK - 1.0 - idx)) * scale
    dq = jnp.broadcast_to(dq[:, :, None], (RET_HEADS, CHUNK, RET_VAL_DIM))
    dk = jnp.broadcast_to(dk[:, :, None], (RET_HEADS, CHUNK, RET_KEY_DIM))
    dc = jnp.exp(log_g * CHUNK)
    return bias, dintra, dq, dk, dc


def _ffn_kernel(x_ref, g_ref, wg_ref, wu_ref, wd_ref, fg_ref, o_ref, *, final):
    x = x_ref[...]
    h = _rms(x, g_ref[...]).astype(BF16)
    acc = x
    for c in range(D_FF // FFN_CH):
        cols = slice(c * FFN_CH, (c + 1) * FFN_CH)
        gate = _dot(h, wg_ref[:, cols].astype(BF16))
        up = _dot(h, wu_ref[:, cols].astype(BF16))
        act = (gate * jax.nn.sigmoid(gate) * up).astype(BF16)
        acc = acc + _dot(act, wd_ref[cols, :].astype(BF16))
    if final:
        acc = _rms(acc, fg_ref[...])
    o_ref[...] = acc


def _ffn(x2, g, wg, wu, wd, fg, layer, final):
    t = x2.shape[0]
    const = lambda i: (0, 0)

    def resident(shape, _):
        return pl.BlockSpec((None,) + tuple(shape[1:]), lambda i: (layer, 0, 0), pipeline_mode=pl.Buffered(1))

    return pl.pallas_call(
        functools.partial(_ffn_kernel, final=final),
        grid=(t // FFN_BM,),
        in_specs=[
            pl.BlockSpec((FFN_BM, D_MODEL), lambda i: (i, 0)),
            pl.BlockSpec((1, D_MODEL), const),
            resident(wg.shape, const),
            resident(wu.shape, const),
            resident(wd.shape, const),
            pl.BlockSpec((1, D_MODEL), const),
        ],
        out_specs=pl.BlockSpec((FFN_BM, D_MODEL), lambda i: (i, 0)),
        out_shape=jax.ShapeDtypeStruct(x2.shape, F32),
        compiler_params=pltpu.CompilerParams(
            dimension_semantics=("arbitrary",), vmem_limit_bytes=VMEM_LIMIT),
        name="ffn_final" if final else "ffn",
    )(x2, g, wg, wu, wd, fg)


def kernel(x, norm_mix, w_in, att_sinks, ret_gn_gain, w_att_o, w_ret_o, w_out, norm_ffn, w_gate, w_up, w_down,
           final_norm):
    batch, seq, d = x.shape
    depth = w_in.shape[0]
    assert d == D_MODEL and seq % MIX_BLK == 0 and (batch * seq) % PROJ_BM == 0 and (batch * seq) % FFN_BM == 0
    x2 = x.reshape(batch * seq, d)
    consts = _mixer_constants()
    fg = final_norm.reshape(1, d)
    for l in range(depth):
        proj, kv = _inproj(x2, norm_mix[l].reshape(1, d), w_in, l)
        sink_row = jnp.repeat(att_sinks[l].astype(F32).reshape(ATT_KV_HEADS, ATT_GROUP), CHUNK, axis=1)[:, None, :]
        x2 = _mixer(proj, kv, x2, consts, sink_row, ret_gn_gain[l].reshape(1, RET_V_W),
                    w_att_o[l].astype(BF16), w_ret_o[l].astype(BF16), w_out[l].astype(BF16), batch, seq)
        x2 = _ffn(x2, norm_ffn[l].reshape(1, d), w_gate, w_up, w_down, fg, l, final=(l == depth - 1))
    return x2.reshape(batch, seq, d)
```

```python
import functools

import jax
import jax.numpy as jnp
from jax import lax
from jax.experimental import pallas as pl
from jax.experimental.pallas import tpu as pltpu

F32 = jnp.float32
BF16 = jnp.bfloat16

D_MODEL = 1024
ATT_HEADS = 8
ATT_KV_HEADS = 2
ATT_HEAD_DIM = 64
ATT_GROUP = ATT_HEADS // ATT_KV_HEADS
WINDOW = 128
CHUNK = 128
RET_HEADS = 4
RET_KEY_DIM = 128
RET_VAL_DIM = 256
D_FF = 2816
EPS = 1e-6

ATT_Q_W = ATT_HEADS * ATT_HEAD_DIM
ATT_KV_W = ATT_KV_HEADS * ATT_HEAD_DIM
RET_QK_W = RET_HEADS * RET_KEY_DIM
RET_V_W = RET_HEADS * RET_VAL_DIM
OFF_AQ = 0
OFF_AK = OFF_AQ + ATT_Q_W
OFF_AV = OFF_AK + ATT_KV_W
OFF_RQ = OFF_AV + ATT_KV_W
OFF_RK = OFF_RQ + RET_QK_W
OFF_RV = OFF_RK + RET_QK_W
OFF_RG = OFF_RV + RET_V_W
OFF_GA = OFF_RG + RET_V_W
OFF_GR = OFF_GA + D_MODEL
D_IN = OFF_GR + D_MODEL

NEG = -1e30

V7X_VMEM_BYTES = 64 * 1024 * 1024
VMEM_LIMIT = 56 * 1024 * 1024

PROJ_BM = 512
PROJ_NCH = 256
MIX_BLK = 512
FFN_BM = 512
FFN_CH = 256


def _rms(x, g):
    return x * lax.rsqrt(jnp.mean(x * x, axis=-1, keepdims=True) + EPS) * g


def _dot(a, b):
    return jnp.dot(a, b, preferred_element_type=F32)


def _dot_nt(a, b):
    return lax.dot_general(a, b, (((1,), (1,)), ((), ())), preferred_element_type=F32)


def _dot_tn(a, b):
    return lax.dot_general(a, b, (((0,), (0,)), ((), ())), preferred_element_type=F32)


def _inproj_kernel(x_ref, g_ref, w_ref, o_ref, kv_ref):
    h = _rms(x_ref[...], g_ref[...]).astype(BF16)
    for c in range(D_IN // PROJ_NCH):
        cols = slice(c * PROJ_NCH, (c + 1) * PROJ_NCH)
        y = _dot(h, w_ref[:, cols].astype(BF16)).astype(BF16)
        o_ref[:, cols] = y
        if c * PROJ_NCH == OFF_AK:
            kv_ref[...] = y


def _inproj(x2, g, w, layer):
    t = x2.shape[0]
    assert PROJ_NCH == 2 * ATT_KV_W and OFF_AK % PROJ_NCH == 0
    return pl.pallas_call(
        _inproj_kernel,
        grid=(t // PROJ_BM,),
        in_specs=[
            pl.BlockSpec((PROJ_BM, D_MODEL), lambda i: (i, 0)),
            pl.BlockSpec((1, D_MODEL), lambda i: (0, 0)),
            pl.BlockSpec((None, D_MODEL, D_IN), lambda i: (layer, 0, 0), pipeline_mode=pl.Buffered(1)),
        ],
        out_specs=[pl.BlockSpec((PROJ_BM, D_IN), lambda i: (i, 0)),
                   pl.BlockSpec((PROJ_BM, 2 * ATT_KV_W), lambda i: (i, 0))],
        out_shape=[jax.ShapeDtypeStruct((t, D_IN), BF16), jax.ShapeDtypeStruct((t, 2 * ATT_KV_W), BF16)],
        compiler_params=pltpu.CompilerParams(
            dimension_semantics=("arbitrary",), vmem_limit_bytes=VMEM_LIMIT),
        name="inproj",
    )(x2, g, w)


def _mixer_kernel(p_ref, kvp_ref, x_ref, bias_ref, sink_ref, dintra_ref, dq_ref, dk_ref, dc_ref,
                  gain_ref, wa_ref, wr_ref, wo_ref, o_ref,
                  state, att_t, ret_s):
    j = pl.program_id(1)

    @pl.when(j == 0)
    def _():
        state[...] = jnp.zeros_like(state)

    first = jnp.where(j == 0, 1, 0)

    for c in range(MIX_BLK // CHUNK):
        rows = slice(c * CHUNK, (c + 1) * CHUNK)
        if c == 0:
            kv_p = kvp_ref[...]
        else:
            kv_p = p_ref[(c - 1) * CHUNK:c * CHUNK, OFF_AK:OFF_AK + 2 * ATT_KV_W]
        kv_c = p_ref[rows, OFF_AK:OFF_AK + 2 * ATT_KV_W]
        kv = jnp.concatenate([kv_p, kv_c], axis=0)
        q = p_ref[rows, OFF_AQ:OFF_AQ + ATT_Q_W]
        for kh in range(ATT_KV_HEADS):
            qs = jnp.concatenate(
                [q[:, (kh * ATT_GROUP + g) * ATT_HEAD_DIM:(kh * ATT_GROUP + g + 1) * ATT_HEAD_DIM]
                 for g in range(ATT_GROUP)], axis=0)
            kk = kv[:, kh * ATT_HEAD_DIM:(kh + 1) * ATT_HEAD_DIM] * (ATT_HEAD_DIM ** -0.5)
            vv = kv[:, ATT_KV_W + kh * ATT_HEAD_DIM:ATT_KV_W + (kh + 1) * ATT_HEAD_DIM]
            bias = bias_ref[first * ATT_KV_HEADS + kh] if c == 0 else bias_ref[kh]
            s = _dot_nt(kk, qs) + bias
            sink = sink_ref[kh]
            m = jnp.maximum(jnp.max(s, axis=0, keepdims=True), sink)
            p = jnp.exp(s - m)
            denom = jnp.sum(p, axis=0, keepdims=True) + jnp.exp(sink - m)
            o = _dot_tn(vv, p.astype(BF16)) / denom
            for g in range(ATT_GROUP):
                h = kh * ATT_GROUP + g
                att_t[h * ATT_HEAD_DIM:(h + 1) * ATT_HEAD_DIM, rows] = o[:, g * CHUNK:(g + 1) * CHUNK].astype(BF16)

        for h in range(RET_HEADS):
            rq = p_ref[rows, OFF_RQ + h * RET_KEY_DIM:OFF_RQ + (h + 1) * RET_KEY_DIM]
            rk = p_ref[rows, OFF_RK + h * RET_KEY_DIM:OFF_RK + (h + 1) * RET_KEY_DIM]
            rv = p_ref[rows, OFF_RV + h * RET_VAL_DIM:OFF_RV + (h + 1) * RET_VAL_DIM]
            st = state[h]
            sc = (_dot_nt(rq, rk) * dintra_ref[h]).astype(BF16)
            qd = (rq.astype(F32) * dq_ref[h]).astype(BF16)
            kd = (rk.astype(F32) * dk_ref[h]).astype(BF16)
            lhs = jnp.concatenate([jnp.concatenate([sc, qd], axis=1),
                                   jnp.concatenate([kd.T, jnp.zeros_like(kd)], axis=1)], axis=0)
            res = _dot(lhs, jnp.concatenate([rv, st.astype(BF16)], axis=0))
            o = res[:CHUNK]
            state[h] = dc_ref[h] * st + res[CHUNK:]
            mu = jnp.mean(o, axis=-1, keepdims=True)
            oc = o - mu
            var = jnp.mean(oc * oc, axis=-1, keepdims=True)
            vcols = slice(h * RET_VAL_DIM, (h + 1) * RET_VAL_DIM)
            y = oc * lax.rsqrt(var + EPS) * gain_ref[:, vcols]
            rg = p_ref[rows, OFF_RG + h * RET_VAL_DIM:OFF_RG + (h + 1) * RET_VAL_DIM].astype(F32)
            ret_s[rows, vcols] = (y * (rg * jax.nn.sigmoid(rg))).astype(BF16)

    a = _dot_tn(att_t[...], wa_ref[...])
    r = _dot(ret_s[...], wr_ref[...])
    ga = p_ref[:, OFF_GA:OFF_GA + D_MODEL].astype(F32)
    gr = p_ref[:, OFF_GR:OFF_GR + D_MODEL].astype(F32)
    merged = (jax.nn.sigmoid(ga) * a + jax.nn.sigmoid(gr) * r).astype(BF16)
    o_ref[...] = x_ref[...] + _dot(merged, wo_ref[...])


def _mixer(proj, kv, x2, consts, sink_row, gain, wa, wr, wo, batch, seq):
    bias, dintra, dq, dk, dc = consts
    nblk = seq // MIX_BLK
    cpb = MIX_BLK // CHUNK
    row = lambda b, j: (b * nblk + j, 0)
    halo = lambda b, j: (jnp.maximum((b * nblk + j) * cpb - 1, 0), 0)
    const2 = lambda b, j: (0, 0)
    const3 = lambda b, j: (0, 0, 0)
    resident = functools.partial(pl.BlockSpec, pipeline_mode=pl.Buffered(1))
    return pl.pallas_call(
        _mixer_kernel,
        grid=(batch, nblk),
        in_specs=[
            pl.BlockSpec((MIX_BLK, D_IN), row),
            pl.BlockSpec((CHUNK, 2 * ATT_KV_W), halo),
            pl.BlockSpec((MIX_BLK, D_MODEL), row),
            resident(bias.shape, const3),
            resident(sink_row.shape, const3),
            resident(dintra.shape, const3),
            resident(dq.shape, const3),
            resident(dk.shape, const3),
            pl.BlockSpec(memory_space=pltpu.SMEM),
            resident(gain.shape, const2),
            resident(wa.shape, const2),
            resident(wr.shape, const2),
            resident(wo.shape, const2),
        ],
        out_specs=pl.BlockSpec((MIX_BLK, D_MODEL), row),
        out_shape=jax.ShapeDtypeStruct(x2.shape, F32),
        scratch_shapes=[
            pltpu.VMEM((RET_HEADS, RET_KEY_DIM, RET_VAL_DIM), F32),
            pltpu.VMEM((ATT_Q_W, MIX_BLK), BF16),
            pltpu.VMEM((MIX_BLK, RET_V_W), BF16),
        ],
        compiler_params=pltpu.CompilerParams(
            dimension_semantics=("arbitrary", "arbitrary"), vmem_limit_bytes=VMEM_LIMIT),
        name="mixer",
    )(proj, kv, x2, bias, sink_row, dintra, dq, dk, dc, gain, wa, wr, wo)


def _mixer_constants():
    i = jnp.arange(CHUNK)[None, :]
    jj = jnp.arange(2 * CHUNK)[:, None]
    dist = i + CHUNK - jj
    valid = (dist >= 0) & (dist < WINDOW)
    slopes = jnp.exp2(-8.0 * (jnp.arange(ATT_HEADS, dtype=F32) + 1.0) / ATT_HEADS)

    def table(ok):
        b = jnp.where(ok[None], -slopes[:, None, None] * dist.astype(F32)[None], NEG)
        b = b.reshape(ATT_KV_HEADS, ATT_GROUP, 2 * CHUNK, CHUNK).transpose(0, 2, 1, 3)
        return b.reshape(ATT_KV_HEADS, 2 * CHUNK, ATT_GROUP * CHUNK)

    bias = jnp.concatenate([table(valid), table(valid & (jj >= CHUNK))], axis=0)

    log_g = jnp.log(1.0 - jnp.exp2(-5.0 - jnp.arange(RET_HEADS, dtype=F32)))
    idx = jnp.arange(CHUNK, dtype=F32)
    diff = idx[:, None] - idx[None, :]
    scale = RET_KEY_DIM ** -0.5
    dintra = jnp.where(diff >= 0, jnp.exp(log_g[:, None, None] * jnp.maximum(diff, 0.0)), 0.0) * scale
    dq = jnp.exp(log_g[:, None] * (idx + 1.0))
    dk = jnp.exp(log_g[:, None] * (CHUNK - 1.0 - idx)) * scale
    dq = jnp.broadcast_to(dq[:, :, None], (RET_HEADS, CHUNK, RET_KEY_DIM))
    dk = jnp.broadcast_to(dk[:, :, None], (RET_HEADS, CHUNK, RET_KEY_DIM))
    dc = jnp.exp(log_g * CHUNK)
    return bias, dintra, dq, dk, dc


def _ffn_kernel(x_ref, g_ref, wg_ref, wu_ref, wd_ref, fg_ref, o_ref, *, final):
    x = x_ref[...]
    h = _rms(x, g_ref[...]).astype(BF16)
    acc = x
    for c in range(D_FF // FFN_CH):
        cols = slice(c * FFN_CH, (c + 1) * FFN_CH)
        gate = _dot(h, wg_ref[:, cols].astype(BF16))
        up = _dot(h, wu_ref[:, cols].astype(BF16))
        act = (gate * jax.nn.sigmoid(gate) * up).astype(BF16)
        acc = acc + _dot(act, wd_ref[cols, :].astype(BF16))
    if final:
        acc = _rms(acc, fg_ref[...])
    o_ref[...] = acc


def _ffn(x2, g, wg, wu, wd, fg, layer, final):
    t = x2.shape[0]
    const = lambda i: (0, 0)

    def resident(shape, _):
        return pl.BlockSpec((None,) + tuple(shape[1:]), lambda i: (layer, 0, 0), pipeline_mode=pl.Buffered(1))

    return pl.pallas_call(
        functools.partial(_ffn_kernel, final=final),
        grid=(t // FFN_BM,),
        in_specs=[
            pl.BlockSpec((FFN_BM, D_MODEL), lambda i: (i, 0)),
            pl.BlockSpec((1, D_MODEL), const),
            resident(wg.shape, const),
            resident(wu.shape, const),
            resident(wd.shape, const),
            pl.BlockSpec((1, D_MODEL), const),
        ],
        out_specs=pl.BlockSpec((FFN_BM, D_MODEL), lambda i: (i, 0)),
        out_shape=jax.ShapeDtypeStruct(x2.shape, F32),
        compiler_params=pltpu.CompilerParams(
            dimension_semantics=("arbitrary",), vmem_limit_bytes=VMEM_LIMIT),
        name="ffn_final" if final else "ffn",
    )(x2, g, wg, wu, wd, fg)


def kernel(x, norm_mix, w_in, att_sinks, ret_gn_gain, w_att_o, w_ret_o, w_out, norm_ffn, w_gate, w_up, w_down,
           final_norm):
    batch, seq, d = x.shape
    depth = w_in.shape[0]
    assert d == D_MODEL and seq % MIX_BLK == 0 and (batch * seq) % PROJ_BM == 0 and (batch * seq) % FFN_BM == 0
    x2 = x.reshape(batch * seq, d)
    consts = _mixer_constants()
    fg = final_norm.reshape(1, d)
    for l in range(depth):
        proj, kv = _inproj(x2, norm_mix[l].reshape(1, d), w_in, l)
        sink_row = jnp.repeat(att_sinks[l].astype(F32).reshape(ATT_KV_HEADS, ATT_GROUP), CHUNK, axis=1)[:, None, :]
        x2 = _mixer(proj, kv, x2, consts, sink_row, ret_gn_gain[l].reshape(1, RET_V_W),
                    w_att_o[l].astype(BF16), w_ret_o[l].astype(BF16), w_out[l].astype(BF16), batch, seq)
        x2 = _ffn(x2, norm_ffn[l].reshape(1, d), w_gate, w_up, w_down, fg, l, final=(l == depth - 1))
    return x2.reshape(batch, seq, d)
```

```python
import functools

import jax
import jax.numpy as jnp
from jax import lax
from jax.experimental import pallas as pl
from jax.experimental.pallas import tpu as pltpu

F32 = jnp.float32
BF16 = jnp.bfloat16

D_MODEL = 1024
ATT_HEADS = 8
ATT_KV_HEADS = 2
ATT_HEAD_DIM = 64
ATT_GROUP = ATT_HEADS // ATT_KV_HEADS
WINDOW = 128
CHUNK = 128
RET_HEADS = 4
RET_KEY_DIM = 128
RET_VAL_DIM = 256
D_FF = 2816
EPS = 1e-6

ATT_Q_W = ATT_HEADS * ATT_HEAD_DIM
ATT_KV_W = ATT_KV_HEADS * ATT_HEAD_DIM
RET_QK_W = RET_HEADS * RET_KEY_DIM
RET_V_W = RET_HEADS * RET_VAL_DIM
OFF_AQ = 0
OFF_AK = OFF_AQ + ATT_Q_W
OFF_AV = OFF_AK + ATT_KV_W
OFF_RQ = OFF_AV + ATT_KV_W
OFF_RK = OFF_RQ + RET_QK_W
OFF_RV = OFF_RK + RET_QK_W
OFF_RG = OFF_RV + RET_V_W
OFF_GA = OFF_RG + RET_V_W
OFF_GR = OFF_GA + D_MODEL
D_IN = OFF_GR + D_MODEL

NEG = -1e30

V7X_VMEM_BYTES = 64 * 1024 * 1024
VMEM_LIMIT = 56 * 1024 * 1024

PROJ_BM = 512
PROJ_NCH = 256
MIX_BLK = 512
FFN_BM = 1024
FFN_CH = 256


def _rms(x, g):
    return x * lax.rsqrt(jnp.mean(x * x, axis=-1, keepdims=True) + EPS) * g


def _dot(a, b):
    return jnp.dot(a, b, preferred_element_type=F32)


def _dot_nt(a, b):
    return lax.dot_general(a, b, (((1,), (1,)), ((), ())), preferred_element_type=F32)


def _dot_tn(a, b):
    return lax.dot_general(a, b, (((0,), (0,)), ((), ())), preferred_element_type=F32)


def _inproj_kernel(x_ref, g_ref, w_ref, o_ref, kv_ref):
    h = _rms(x_ref[...], g_ref[...]).astype(BF16)
    for c in range(D_IN // PROJ_NCH):
        cols = slice(c * PROJ_NCH, (c + 1) * PROJ_NCH)
        y = _dot(h, w_ref[:, cols].astype(BF16)).astype(BF16)
        o_ref[:, cols] = y
        if c * PROJ_NCH == OFF_AK:
            kv_ref[...] = y


def _inproj(x2, g, w, layer):
    t = x2.shape[0]
    assert PROJ_NCH == 2 * ATT_KV_W and OFF_AK % PROJ_NCH == 0
    return pl.pallas_call(
        _inproj_kernel,
        grid=(t // PROJ_BM,),
        in_specs=[
            pl.BlockSpec((PROJ_BM, D_MODEL), lambda i: (i, 0)),
            pl.BlockSpec((1, D_MODEL), lambda i: (0, 0)),
            pl.BlockSpec((None, D_MODEL, D_IN), lambda i: (layer, 0, 0), pipeline_mode=pl.Buffered(1)),
        ],
        out_specs=[pl.BlockSpec((PROJ_BM, D_IN), lambda i: (i, 0)),
                   pl.BlockSpec((PROJ_BM, 2 * ATT_KV_W), lambda i: (i, 0))],
        out_shape=[jax.ShapeDtypeStruct((t, D_IN), BF16), jax.ShapeDtypeStruct((t, 2 * ATT_KV_W), BF16)],
        compiler_params=pltpu.CompilerParams(
            dimension_semantics=("arbitrary",), vmem_limit_bytes=VMEM_LIMIT),
        name="inproj",
    )(x2, g, w)


def _mixer_kernel(p_ref, kvp_ref, x_ref, bias_ref, sink_ref, dintra_ref, dq_ref, dk_ref, dc_ref,
                  gain_ref, wa_ref, wr_ref, wo_ref, o_ref,
                  state, att_t, ret_s):
    j = pl.program_id(1)

    @pl.when(j == 0)
    def _():
        state[...] = jnp.zeros_like(state)

    first = jnp.where(j == 0, 1, 0)

    for c in range(MIX_BLK // CHUNK):
        rows = slice(c * CHUNK, (c + 1) * CHUNK)
        if c == 0:
            kv_p = kvp_ref[...]
        else:
            kv_p = p_ref[(c - 1) * CHUNK:c * CHUNK, OFF_AK:OFF_AK + 2 * ATT_KV_W]
        kv_c = p_ref[rows, OFF_AK:OFF_AK + 2 * ATT_KV_W]
        kv = jnp.concatenate([kv_p, kv_c], axis=0)
        q = p_ref[rows, OFF_AQ:OFF_AQ + ATT_Q_W]
        for kh in range(ATT_KV_HEADS):
            qs = jnp.concatenate(
                [q[:, (kh * ATT_GROUP + g) * ATT_HEAD_DIM:(kh * ATT_GROUP + g + 1) * ATT_HEAD_DIM]
                 for g in range(ATT_GROUP)], axis=0)
            kk = kv[:, kh * ATT_HEAD_DIM:(kh + 1) * ATT_HEAD_DIM] * (ATT_HEAD_DIM ** -0.5)
            vv = kv[:, ATT_KV_W + kh * ATT_HEAD_DIM:ATT_KV_W + (kh + 1) * ATT_HEAD_DIM]
            bias = bias_ref[first * ATT_KV_HEADS + kh] if c == 0 else bias_ref[kh]
            s = _dot_nt(kk, qs) + bias
            sink = sink_ref[kh]
            m = jnp.maximum(jnp.max(s, axis=0, keepdims=True), sink)
            p = jnp.exp(s - m)
            denom = jnp.sum(p, axis=0, keepdims=True) + jnp.exp(sink - m)
            o = _dot_tn(vv, p.astype(BF16)) / denom
            for g in range(ATT_GROUP):
                h = kh * ATT_GROUP + g
                att_t[h * ATT_HEAD_DIM:(h + 1) * ATT_HEAD_DIM, rows] = o[:, g * CHUNK:(g + 1) * CHUNK].astype(BF16)

        for h in range(RET_HEADS):
            rq = p_ref[rows, OFF_RQ + h * RET_KEY_DIM:OFF_RQ + (h + 1) * RET_KEY_DIM]
            rk = p_ref[rows, OFF_RK + h * RET_KEY_DIM:OFF_RK + (h + 1) * RET_KEY_DIM]
            rv = p_ref[rows, OFF_RV + h * RET_VAL_DIM:OFF_RV + (h + 1) * RET_VAL_DIM]
            st = state[h]
            sc = (_dot_nt(rq, rk) * dintra_ref[h]).astype(BF16)
            qd = (rq.astype(F32) * dq_ref[h]).astype(BF16)
            kd = (rk.astype(F32) * dk_ref[h]).astype(BF16)
            lhs = jnp.concatenate([jnp.concatenate([sc, qd], axis=1),
                                   jnp.concatenate([kd.T, jnp.zeros_like(kd)], axis=1)], axis=0)
            res = _dot(lhs, jnp.concatenate([rv, st.astype(BF16)], axis=0))
            o = res[:CHUNK]
            state[h] = dc_ref[h] * st + res[CHUNK:]
            mu = jnp.mean(o, axis=-1, keepdims=True)
            oc = o - mu
            var = jnp.mean(oc * oc, axis=-1, keepdims=True)
            vcols = slice(h * RET_VAL_DIM, (h + 1) * RET_VAL_DIM)
            y = oc * lax.rsqrt(var + EPS) * gain_ref[:, vcols]
            rg = p_ref[rows, OFF_RG + h * RET_VAL_DIM:OFF_RG + (h + 1) * RET_VAL_DIM].astype(F32)
            ret_s[rows, vcols] = (y * (rg * jax.nn.sigmoid(rg))).astype(BF16)

    a = _dot_tn(att_t[...], wa_ref[...])
    r = _dot(ret_s[...], wr_ref[...])
    ga = p_ref[:, OFF_GA:OFF_GA + D_MODEL].astype(F32)
    gr = p_ref[:, OFF_GR:OFF_GR + D_MODEL].astype(F32)
    merged = (jax.nn.sigmoid(ga) * a + jax.nn.sigmoid(gr) * r).astype(BF16)
    o_ref[...] = x_ref[...] + _dot(merged, wo_ref[...])


def _mixer(proj, kv, x2, consts, sink_row, gain, wa, wr, wo, batch, seq):
    bias, dintra, dq, dk, dc = consts
    nblk = seq // MIX_BLK
    cpb = MIX_BLK // CHUNK
    row = lambda b, j: (b * nblk + j, 0)
    halo = lambda b, j: (jnp.maximum((b * nblk + j) * cpb - 1, 0), 0)
    const2 = lambda b, j: (0, 0)
    const3 = lambda b, j: (0, 0, 0)
    resident = functools.partial(pl.BlockSpec, pipeline_mode=pl.Buffered(1))
    return pl.pallas_call(
        _mixer_kernel,
        grid=(batch, nblk),
        in_specs=[
            pl.BlockSpec((MIX_BLK, D_IN), row),
            pl.BlockSpec((CHUNK, 2 * ATT_KV_W), halo),
            pl.BlockSpec((MIX_BLK, D_MODEL), row),
            resident(bias.shape, const3),
            resident(sink_row.shape, const3),
            resident(dintra.shape, const3),
            resident(dq.shape, const3),
            resident(dk.shape, const3),
            pl.BlockSpec(memory_space=pltpu.SMEM),
            resident(gain.shape, const2),
            resident(wa.shape, const2),
            resident(wr.shape, const2),
            resident(wo.shape, const2),
        ],
        out_specs=pl.BlockSpec((MIX_BLK, D_MODEL), row),
        out_shape=jax.ShapeDtypeStruct(x2.shape, F32),
        scratch_shapes=[
            pltpu.VMEM((RET_HEADS, RET_KEY_DIM, RET_VAL_DIM), F32),
            pltpu.VMEM((ATT_Q_W, MIX_BLK), BF16),
            pltpu.VMEM((MIX_BLK, RET_V_W), BF16),
        ],
        compiler_params=pltpu.CompilerParams(
            dimension_semantics=("arbitrary", "arbitrary"), vmem_limit_bytes=VMEM_LIMIT),
        name="mixer",
    )(proj, kv, x2, bias, sink_row, dintra, dq, dk, dc, gain, wa, wr, wo)


def _mixer_constants():
    i = jnp.arange(CHUNK)[None, :]
    jj = jnp.arange(2 * CHUNK)[:, None]
    dist = i + CHUNK - jj
    valid = (dist >= 0) & (dist < WINDOW)
    slopes = jnp.exp2(-8.0 * (jnp.arange(ATT_HEADS, dtype=F32) + 1.0) / ATT_HEADS)

    def table(ok):
        b = jnp.where(ok[None], -slopes[:, None, None] * dist.astype(F32)[None], NEG)
        b = b.reshape(ATT_KV_HEADS, ATT_GROUP, 2 * CHUNK, CHUNK).transpose(0, 2, 1, 3)
        return b.reshape(ATT_KV_HEADS, 2 * CHUNK, ATT_GROUP * CHUNK)

    bias = jnp.concatenate([table(valid), table(valid & (jj >= CHUNK))], axis=0)

    log_g = jnp.log(1.0 - jnp.exp2(-5.0 - jnp.arange(RET_HEADS, dtype=F32)))
    idx = jnp.arange(CHUNK, dtype=F32)
    diff = idx[:, None] - idx[None, :]
    scale = RET_KEY_DIM ** -0.5
    dintra = jnp.where(diff >= 0, jnp.exp(log_g[:, None, None] * jnp.maximum(diff, 0.0)), 0.0) * scale
    dq = jnp.exp(log_g[:, None] * (idx + 1.0))
    dk = jnp.exp(log_g[:, None] * (CHUNK - 1.0 - idx)) * scale
    dq = jnp.broadcast_to(dq[:, :, None], (RET_HEADS, CHUNK, RET_KEY_DIM))
    dk = jnp.broadcast_to(dk[:, :, None], (RET_HEADS, CHUNK, RET_KEY_DIM))
    dc = jnp.exp(log_g * CHUNK)
    return bias, dintra, dq, dk, dc


def _ffn_kernel(x_ref, g_ref, wg_ref, wu_ref, wd_ref, fg_ref, o_ref, *, final):
    x = x_ref[...]
    h = _rms(x, g_ref[...]).astype(BF16)
    acc = x
    for c in range(D_FF // FFN_CH):
        cols = slice(c * FFN_CH, (c + 1) * FFN_CH)
        gate = _dot(h, wg_ref[:, cols].astype(BF16))
        up = _dot(h, wu_ref[:, cols].astype(BF16))
        act = (gate * jax.nn.sigmoid(gate) * up).astype(BF16)
        acc = acc + _dot(act, wd_ref[cols, :].astype(BF16))
    if final:
        acc = _rms(acc, fg_ref[...])
    o_ref[...] = acc


def _ffn(x2, g, wg, wu, wd, fg, layer, final):
    t = x2.shape[0]
    const = lambda i: (0, 0)

    def resident(shape, _):
        return pl.BlockSpec((None,) + tuple(shape[1:]), lambda i: (layer, 0, 0), pipeline_mode=pl.Buffered(1))

    return pl.pallas_call(
        functools.partial(_ffn_kernel, final=final),
        grid=(t // FFN_BM,),
        in_specs=[
            pl.BlockSpec((FFN_BM, D_MODEL), lambda i: (i, 0)),
            pl.BlockSpec((1, D_MODEL), const),
            resident(wg.shape, const),
            resident(wu.shape, const),
            resident(wd.shape, const),
            pl.BlockSpec((1, D_MODEL), const),
        ],
        out_specs=pl.BlockSpec((FFN_BM, D_MODEL), lambda i: (i, 0)),
        out_shape=jax.ShapeDtypeStruct(x2.shape, F32),
        compiler_params=pltpu.CompilerParams(
            dimension_semantics=("arbitrary",), vmem_limit_bytes=VMEM_LIMIT),
        name="ffn_final" if final else "ffn",
    )(x2, g, wg, wu, wd, fg)


def kernel(x, norm_mix, w_in, att_sinks, ret_gn_gain, w_att_o, w_ret_o, w_out, norm_ffn, w_gate, w_up, w_down,
           final_norm):
    batch, seq, d = x.shape
    depth = w_in.shape[0]
    assert d == D_MODEL and seq % MIX_BLK == 0 and (batch * seq) % PROJ_BM == 0 and (batch * seq) % FFN_BM == 0
    x2 = x.reshape(batch * seq, d)
    consts = _mixer_constants()
    fg = final_norm.reshape(1, d)
    for l in range(depth):
        proj, kv = _inproj(x2, norm_mix[l].reshape(1, d), w_in, l)
        sink_row = jnp.repeat(att_sinks[l].astype(F32).reshape(ATT_KV_HEADS, ATT_GROUP), CHUNK, axis=1)[:, None, :]
        x2 = _mixer(proj, kv, x2, consts, sink_row, ret_gn_gain[l].reshape(1, RET_V_W),
                    w_att_o[l].astype(BF16), w_ret_o[l].astype(BF16), w_out[l].astype(BF16), batch, seq)
        x2 = _ffn(x2, norm_ffn[l].reshape(1, d), w_gate, w_up, w_down, fg, l, final=(l == depth - 1))
    return x2.reshape(batch, seq, d)
```

```python
import functools

import jax
import jax.numpy as jnp
from jax import lax
from jax.experimental import pallas as pl
from jax.experimental.pallas import tpu as pltpu

F32 = jnp.float32
BF16 = jnp.bfloat16

D_MODEL = 1024
ATT_HEADS = 8
ATT_KV_HEADS = 2
ATT_HEAD_DIM = 64
ATT_GROUP = ATT_HEADS // ATT_KV_HEADS
WINDOW = 128
CHUNK = 128
RET_HEADS = 4
RET_KEY_DIM = 128
RET_VAL_DIM = 256
D_FF = 2816
EPS = 1e-6

ATT_Q_W = ATT_HEADS * ATT_HEAD_DIM
ATT_KV_W = ATT_KV_HEADS * ATT_HEAD_DIM
RET_QK_W = RET_HEADS * RET_KEY_DIM
RET_V_W = RET_HEADS * RET_VAL_DIM
OFF_AQ = 0
OFF_AK = OFF_AQ + ATT_Q_W
OFF_AV = OFF_AK + ATT_KV_W
OFF_RQ = OFF_AV + ATT_KV_W
OFF_RK = OFF_RQ + RET_QK_W
OFF_RV = OFF_RK + RET_QK_W
OFF_RG = OFF_RV + RET_V_W
OFF_GA = OFF_RG + RET_V_W
OFF_GR = OFF_GA + D_MODEL
D_IN = OFF_GR + D_MODEL

NEG = -1e30

V7X_VMEM_BYTES = 64 * 1024 * 1024
VMEM_LIMIT = 56 * 1024 * 1024

PROJ_BM = 512
PROJ_NCH = 256
MIX_BLK = 512
FFN_BM = 1024
FFN_CH = 256


def _rms(x, g):
    return x * lax.rsqrt(jnp.mean(x * x, axis=-1, keepdims=True) + EPS) * g


def _dot(a, b):
    return jnp.dot(a, b, preferred_element_type=F32)


def _dot_nt(a, b):
    return lax.dot_general(a, b, (((1,), (1,)), ((), ())), preferred_element_type=F32)


def _dot_tn(a, b):
    return lax.dot_general(a, b, (((0,), (0,)), ((), ())), preferred_element_type=F32)


def _inproj_kernel(x_ref, g_ref, w_ref, o_ref):
    h = _rms(x_ref[...], g_ref[...]).astype(BF16)
    for c in range(D_IN // PROJ_NCH):
        cols = slice(c * PROJ_NCH, (c + 1) * PROJ_NCH)
        o_ref[:, cols] = _dot(h, w_ref[:, cols].astype(BF16)).astype(BF16)


def _inproj(x2, g, w, layer):
    t = x2.shape[0]
    return pl.pallas_call(
        _inproj_kernel,
        grid=(t // PROJ_BM,),
        in_specs=[
            pl.BlockSpec((PROJ_BM, D_MODEL), lambda i: (i, 0)),
            pl.BlockSpec((1, D_MODEL), lambda i: (0, 0)),
            pl.BlockSpec((None, D_MODEL, D_IN), lambda i: (layer, 0, 0), pipeline_mode=pl.Buffered(1)),
        ],
        out_specs=pl.BlockSpec((PROJ_BM, D_IN), lambda i: (i, 0)),
        out_shape=jax.ShapeDtypeStruct((t, D_IN), BF16),
        compiler_params=pltpu.CompilerParams(
            dimension_semantics=("arbitrary",), vmem_limit_bytes=VMEM_LIMIT),
        name="inproj",
    )(x2, g, w)


def _mixer_kernel(p_ref, kvp_ref, x_ref, bias_ref, sink_ref, dintra_ref, dq_ref, dk_ref, dc_ref,
                  gain_ref, wa_ref, wr_ref, wo_ref, o_ref,
                  state, att_t, ret_s):
    j = pl.program_id(1)

    @pl.when(j == 0)
    def _():
        state[...] = jnp.zeros_like(state)

    first = jnp.where(j == 0, 1, 0)

    for c in range(MIX_BLK // CHUNK):
        rows = slice(c * CHUNK, (c + 1) * CHUNK)
        if c == 0:
            kv_p = kvp_ref[...]
        else:
            kv_p = p_ref[(c - 1) * CHUNK:c * CHUNK, OFF_AK:OFF_AK + 2 * ATT_KV_W]
        kv_c = p_ref[rows, OFF_AK:OFF_AK + 2 * ATT_KV_W]
        kv = jnp.concatenate([kv_p, kv_c], axis=0)
        q = p_ref[rows, OFF_AQ:OFF_AQ + ATT_Q_W]
        for kh in range(ATT_KV_HEADS):
            qs = jnp.concatenate(
                [q[:, (kh * ATT_GROUP + g) * ATT_HEAD_DIM:(kh * ATT_GROUP + g + 1) * ATT_HEAD_DIM]
                 for g in range(ATT_GROUP)], axis=0)
            kk = kv[:, kh * ATT_HEAD_DIM:(kh + 1) * ATT_HEAD_DIM] * (ATT_HEAD_DIM ** -0.5)
            vv = kv[:, ATT_KV_W + kh * ATT_HEAD_DIM:ATT_KV_W + (kh + 1) * ATT_HEAD_DIM]
            bias = bias_ref[first * ATT_KV_HEADS + kh] if c == 0 else bias_ref[kh]
            s = _dot_nt(kk, qs) + bias
            sink = sink_ref[kh]
            m = jnp.maximum(jnp.max(s, axis=0, keepdims=True), sink)
            p = jnp.exp(s - m)
            denom = jnp.sum(p, axis=0, keepdims=True) + jnp.exp(sink - m)
            o = _dot_tn(vv, p.astype(BF16)) / denom
            for g in range(ATT_GROUP):
                h = kh * ATT_GROUP + g
                att_t[h * ATT_HEAD_DIM:(h + 1) * ATT_HEAD_DIM, rows] = o[:, g * CHUNK:(g + 1) * CHUNK].astype(BF16)

        for h in range(RET_HEADS):
            rq = p_ref[rows, OFF_RQ + h * RET_KEY_DIM:OFF_RQ + (h + 1) * RET_KEY_DIM]
            rk = p_ref[rows, OFF_RK + h * RET_KEY_DIM:OFF_RK + (h + 1) * RET_KEY_DIM]
            rv = p_ref[rows, OFF_RV + h * RET_VAL_DIM:OFF_RV + (h + 1) * RET_VAL_DIM]
            st = state[h]
            sc = (_dot_nt(rq, rk) * dintra_ref[h]).astype(BF16)
            qd = (rq.astype(F32) * dq_ref[h]).astype(BF16)
            kd = (rk.astype(F32) * dk_ref[h]).astype(BF16)
            lhs = jnp.concatenate([jnp.concatenate([sc, qd], axis=1),
                                   jnp.concatenate([kd.T, jnp.zeros_like(kd)], axis=1)], axis=0)
            res = _dot(lhs, jnp.concatenate([rv, st.astype(BF16)], axis=0))
            o = res[:CHUNK]
            state[h] = dc_ref[h] * st + res[CHUNK:]
            mu = jnp.mean(o, axis=-1, keepdims=True)
            oc = o - mu
            var = jnp.mean(oc * oc, axis=-1, keepdims=True)
            vcols = slice(h * RET_VAL_DIM, (h + 1) * RET_VAL_DIM)
            y = oc * lax.rsqrt(var + EPS) * gain_ref[:, vcols]
            rg = p_ref[rows, OFF_RG + h * RET_VAL_DIM:OFF_RG + (h + 1) * RET_VAL_DIM].astype(F32)
            ret_s[rows, vcols] = (y * (rg * jax.nn.sigmoid(rg))).astype(BF16)

    a = _dot_tn(att_t[...], wa_ref[...])
    r = _dot(ret_s[...], wr_ref[...])
    ga = p_ref[:, OFF_GA:OFF_GA + D_MODEL].astype(F32)
    gr = p_ref[:, OFF_GR:OFF_GR + D_MODEL].astype(F32)
    merged = (jax.nn.sigmoid(ga) * a + jax.nn.sigmoid(gr) * r).astype(BF16)
    o_ref[...] = x_ref[...] + _dot(merged, wo_ref[...])


def _mixer(proj, x2, consts, sink_row, gain, wa, wr, wo, batch, seq):
    bias, dintra, dq, dk, dc = consts
    nblk = seq // MIX_BLK
    cpb = MIX_BLK // CHUNK
    kv_w = 2 * ATT_KV_W
    assert OFF_AK % kv_w == 0
    row = lambda b, j: (b * nblk + j, 0)
    halo = lambda b, j: (jnp.maximum((b * nblk + j) * cpb - 1, 0), OFF_AK // kv_w)
    const2 = lambda b, j: (0, 0)
    const3 = lambda b, j: (0, 0, 0)
    resident = functools.partial(pl.BlockSpec, pipeline_mode=pl.Buffered(1))
    return pl.pallas_call(
        _mixer_kernel,
        grid=(batch, nblk),
        in_specs=[
            pl.BlockSpec((MIX_BLK, D_IN), row),
            pl.BlockSpec((CHUNK, kv_w), halo),
            pl.BlockSpec((MIX_BLK, D_MODEL), row),
            resident(bias.shape, const3),
            resident(sink_row.shape, const3),
            resident(dintra.shape, const3),
            resident(dq.shape, const3),
            resident(dk.shape, const3),
            pl.BlockSpec(memory_space=pltpu.SMEM),
            resident(gain.shape, const2),
            resident(wa.shape, const2),
            resident(wr.shape, const2),
            resident(wo.shape, const2),
        ],
        out_specs=pl.BlockSpec((MIX_BLK, D_MODEL), row),
        out_shape=jax.ShapeDtypeStruct(x2.shape, F32),
        scratch_shapes=[
            pltpu.VMEM((RET_HEADS, RET_KEY_DIM, RET_VAL_DIM), F32),
            pltpu.VMEM((ATT_Q_W, MIX_BLK), BF16),
            pltpu.VMEM((MIX_BLK, RET_V_W), BF16),
        ],
        compiler_params=pltpu.CompilerParams(
            dimension_semantics=("arbitrary", "arbitrary"), vmem_limit_bytes=VMEM_LIMIT),
        name="mixer",
    )(proj, proj, x2, bias, sink_row, dintra, dq, dk, dc, gain, wa, wr, wo)


def _mixer_constants():
    i = jnp.arange(CHUNK)[None, :]
    jj = jnp.arange(2 * CHUNK)[:, None]
    dist = i + CHUNK - jj
    valid = (dist >= 0) & (dist < WINDOW)
    slopes = jnp.exp2(-8.0 * (jnp.arange(ATT_HEADS, dtype=F32) + 1.0) / ATT_HEADS)

    def table(ok):
        b = jnp.where(ok[None], -slopes[:, None, None] * dist.astype(F32)[None], NEG)
        b = b.reshape(ATT_KV_HEADS, ATT_GROUP, 2 * CHUNK, CHUNK).transpose(0, 2, 1, 3)
        return b.reshape(ATT_KV_HEADS, 2 * CHUNK, ATT_GROUP * CHUNK)

    bias = jnp.concatenate([table(valid), table(valid & (jj >= CHUNK))], axis=0)

    log_g = jnp.log(1.0 - jnp.exp2(-5.0 - jnp.arange(RET_HEADS, dtype=F32)))
    idx = jnp.arange(CHUNK, dtype=F32)
    diff = idx[:, None] - idx[None, :]
    scale = RET_KEY_DIM ** -0.5
    dintra = jnp.where(diff >= 0, jnp.exp(log_g[:, None, None] * jnp.maximum(diff, 0.0)), 0.0) * scale
    dq = jnp.exp(log_g[:, None] * (idx + 1.0))
    dk = jnp.exp(log_g[:, None] * (CHUNK - 1.0 - idx)) * scale
    dq = jnp.broadcast_to(dq[:, :, None], (RET_HEADS, CHUNK, RET_KEY_DIM))
    dk = jnp.broadcast_to(dk[:, :, None], (RET_HEADS, CHUNK, RET_KEY_DIM))
    dc = jnp.exp(log_g * CHUNK)
    return bias, dintra, dq, dk, dc


def _ffn_kernel(x_ref, g_ref, wg_ref, wu_ref, wd_ref, fg_ref, o_ref, *, final):
    x = x_ref[...]
    h = _rms(x, g_ref[...]).astype(BF16)
    acc = x
    for c in range(D_FF // FFN_CH):
        cols = slice(c * FFN_CH, (c + 1) * FFN_CH)
        gate = _dot(h, wg_ref[:, cols].astype(BF16))
        up = _dot(h, wu_ref[:, cols].astype(BF16))
        act = (gate * jax.nn.sigmoid(gate) * up).astype(BF16)
        acc = acc + _dot(act, wd_ref[cols, :].astype(BF16))
    if final:
        acc = _rms(acc, fg_ref[...])
    o_ref[...] = acc


def _ffn(x2, g, wg, wu, wd, fg, layer, final):
    t = x2.shape[0]
    const = lambda i: (0, 0)

    def resident(shape, _):
        return pl.BlockSpec((None,) + tuple(shape[1:]), lambda i: (layer, 0, 0), pipeline_mode=pl.Buffered(1))

    return pl.pallas_call(
        functools.partial(_ffn_kernel, final=final),
        grid=(t // FFN_BM,),
        in_specs=[
            pl.BlockSpec((FFN_BM, D_MODEL), lambda i: (i, 0)),
            pl.BlockSpec((1, D_MODEL), const),
            resident(wg.shape, const),
            resident(wu.shape, const),
            resident(wd.shape, const),
            pl.BlockSpec((1, D_MODEL), const),
        ],
        out_specs=pl.BlockSpec((FFN_BM, D_MODEL), lambda i: (i, 0)),
        out_shape=jax.ShapeDtypeStruct(x2.shape, F32),
        compiler_params=pltpu.CompilerParams(
            dimension_semantics=("arbitrary",), vmem_limit_bytes=VMEM_LIMIT),
        name="ffn_final" if final else "ffn",
    )(x2, g, wg, wu, wd, fg)


def kernel(x, norm_mix, w_in, att_sinks, ret_gn_gain, w_att_o, w_ret_o, w_out, norm_ffn, w_gate, w_up, w_down,
           final_norm):
    batch, seq, d = x.shape
    depth = w_in.shape[0]
    assert d == D_MODEL and seq % MIX_BLK == 0 and (batch * seq) % PROJ_BM == 0 and (batch * seq) % FFN_BM == 0
    x2 = x.reshape(batch * seq, d)
    consts = _mixer_constants()
    fg = final_norm.reshape(1, d)
    for l in range(depth):
        proj = _inproj(x2, norm_mix[l].reshape(1, d), w_in, l)
        sink_row = jnp.repeat(att_sinks[l].astype(F32).reshape(ATT_KV_HEADS, ATT_GROUP), CHUNK, axis=1)[:, None, :]
        x2 = _mixer(proj, x2, consts, sink_row, ret_gn_gain[l].reshape(1, RET_V_W),
                    w_att_o[l].astype(BF16), w_ret_o[l].astype(BF16), w_out[l].astype(BF16), batch, seq)
        x2 = _ffn(x2, norm_ffn[l].reshape(1, d), w_gate, w_up, w_down, fg, l, final=(l == depth - 1))
    return x2.reshape(batch, seq, d)
```

```python
import functools

import jax
import jax.numpy as jnp
import numpy as np
from jax import lax
from jax.experimental import pallas as pl
from jax.experimental.pallas import tpu as pltpu

F32 = jnp.float32
BF16 = jnp.bfloat16

D_MODEL = 1024
ATT_HEADS = 8
ATT_KV_HEADS = 2
ATT_HEAD_DIM = 64
ATT_GROUP = ATT_HEADS // ATT_KV_HEADS
WINDOW = 128
CHUNK = 128
RET_HEADS = 4
RET_KEY_DIM = 128
RET_VAL_DIM = 256
D_FF = 2816
EPS = 1e-6

ATT_Q_W = ATT_HEADS * ATT_HEAD_DIM
ATT_KV_W = ATT_KV_HEADS * ATT_HEAD_DIM
RET_QK_W = RET_HEADS * RET_KEY_DIM
RET_V_W = RET_HEADS * RET_VAL_DIM
OFF_AQ = 0
OFF_AK = OFF_AQ + ATT_Q_W
OFF_AV = OFF_AK + ATT_KV_W
OFF_RQ = OFF_AV + ATT_KV_W
OFF_RK = OFF_RQ + RET_QK_W
OFF_RV = OFF_RK + RET_QK_W
OFF_RG = OFF_RV + RET_V_W
OFF_GA = OFF_RG + RET_V_W
OFF_GR = OFF_GA + D_MODEL
D_IN = OFF_GR + D_MODEL

NEG = -1e30
SUM_ROWS = 8

V7X_VMEM_BYTES = 64 * 1024 * 1024
VMEM_LIMIT = V7X_VMEM_BYTES - 8 * 1024 * 1024

PROJ_BM = 512
PROJ_NCH = 256
MIX_BLK = 512
FFN_BM = 1024
FFN_CH = 256


def _rms(x, g):
    return x * lax.rsqrt(jnp.mean(x * x, axis=-1, keepdims=True) + EPS) * g


def _dot(a, b):
    return jnp.dot(a, b, preferred_element_type=F32)


def _dot_nt(a, b):
    return lax.dot_general(a, b, (((1,), (1,)), ((), ())), preferred_element_type=F32)


def _dot_tn(a, b):
    return lax.dot_general(a, b, (((0,), (0,)), ((), ())), preferred_element_type=F32)


def _inproj_kernel(x_ref, g_ref, w_ref, o_ref):
    h = _rms(x_ref[...], g_ref[...]).astype(BF16)
    for c in range(D_IN // PROJ_NCH):
        cols = slice(c * PROJ_NCH, (c + 1) * PROJ_NCH)
        o_ref[:, cols] = _dot(h, w_ref[:, cols].astype(BF16)).astype(BF16)


def _inproj(x2, g, w, layer):
    t = x2.shape[0]
    return pl.pallas_call(
        _inproj_kernel,
        grid=(t // PROJ_BM,),
        in_specs=[
            pl.BlockSpec((PROJ_BM, D_MODEL), lambda i: (i, 0)),
            pl.BlockSpec((1, D_MODEL), lambda i: (0, 0)),
            pl.BlockSpec((None, D_MODEL, D_IN), lambda i: (layer, 0, 0), pipeline_mode=pl.Buffered(1)),
        ],
        out_specs=pl.BlockSpec((PROJ_BM, D_IN), lambda i: (i, 0)),
        out_shape=jax.ShapeDtypeStruct((t, D_IN), BF16),
        compiler_params=pltpu.CompilerParams(
            dimension_semantics=("arbitrary",), vmem_limit_bytes=VMEM_LIMIT),
        name="inproj",
    )(x2, g, w)


def _mixer_kernel(p_ref, kvp_ref, x_ref, bias_ref, sink_ref, dintra_ref, dq_ref, dk_ref, dc_ref,
                  gain_ref, wa_ref, wr_ref, wo_ref, o_ref,
                  state, att_t, ret_s):
    j = pl.program_id(1)

    @pl.when(j == 0)
    def _():
        state[...] = jnp.zeros_like(state)

    first = jnp.where(j == 0, 1, 0)

    for c in range(MIX_BLK // CHUNK):
        rows = slice(c * CHUNK, (c + 1) * CHUNK)
        if c == 0:
            kv_p = kvp_ref[...]
        else:
            kv_p = p_ref[(c - 1) * CHUNK:c * CHUNK, OFF_AK:OFF_AK + 2 * ATT_KV_W]
        kv_c = p_ref[rows, OFF_AK:OFF_AK + 2 * ATT_KV_W]
        kv = jnp.concatenate([kv_p, kv_c], axis=0)
        q = p_ref[rows, OFF_AQ:OFF_AQ + ATT_Q_W]
        for kh in range(ATT_KV_HEADS):
            qs = jnp.concatenate(
                [q[:, (kh * ATT_GROUP + g) * ATT_HEAD_DIM:(kh * ATT_GROUP + g + 1) * ATT_HEAD_DIM]
                 for g in range(ATT_GROUP)], axis=0)
            kk = kv[:, kh * ATT_HEAD_DIM:(kh + 1) * ATT_HEAD_DIM] * (ATT_HEAD_DIM ** -0.5)
            vv = kv[:, ATT_KV_W + kh * ATT_HEAD_DIM:ATT_KV_W + (kh + 1) * ATT_HEAD_DIM]
            bias = bias_ref[first * ATT_KV_HEADS + kh] if c == 0 else bias_ref[kh]
            s = _dot_nt(kk, qs) + bias
            sink = sink_ref[kh]
            m = jnp.maximum(jnp.max(s, axis=0, keepdims=True), sink)
            p = jnp.exp((s - m).astype(BF16))
            ov = _dot_tn(jnp.concatenate([vv, jnp.ones((2 * CHUNK, SUM_ROWS), BF16)], axis=1), p)
            denom = ov[ATT_HEAD_DIM:ATT_HEAD_DIM + 1] + jnp.exp(sink - m)
            o = ov[:ATT_HEAD_DIM] / denom
            for g in range(ATT_GROUP):
                h = kh * ATT_GROUP + g
                att_t[h * ATT_HEAD_DIM:(h + 1) * ATT_HEAD_DIM, rows] = o[:, g * CHUNK:(g + 1) * CHUNK].astype(BF16)

        for h in range(RET_HEADS):
            rq = p_ref[rows, OFF_RQ + h * RET_KEY_DIM:OFF_RQ + (h + 1) * RET_KEY_DIM]
            rk = p_ref[rows, OFF_RK + h * RET_KEY_DIM:OFF_RK + (h + 1) * RET_KEY_DIM]
            rv = p_ref[rows, OFF_RV + h * RET_VAL_DIM:OFF_RV + (h + 1) * RET_VAL_DIM]
            st = state[h]
            sc = (_dot_nt(rq, rk) * dintra_ref[h]).astype(BF16)
            qd = (rq.astype(F32) * dq_ref[h]).astype(BF16)
            kd = (rk.astype(F32) * dk_ref[h]).astype(BF16)
            lhs = jnp.concatenate([jnp.concatenate([sc, qd], axis=1),
                                   jnp.concatenate([kd.T, jnp.zeros_like(kd)], axis=1)], axis=0)
            res = _dot(lhs, jnp.concatenate([rv, st.astype(BF16)], axis=0))
            o = res[:CHUNK]
            state[h] = dc_ref[h] * st + res[CHUNK:]
            mu = jnp.mean(o, axis=-1, keepdims=True)
            oc = o - mu
            var = jnp.mean(oc * oc, axis=-1, keepdims=True)
            vcols = slice(h * RET_VAL_DIM, (h + 1) * RET_VAL_DIM)
            y = oc * lax.rsqrt(var + EPS) * gain_ref[:, vcols]
            rg = p_ref[rows, OFF_RG + h * RET_VAL_DIM:OFF_RG + (h + 1) * RET_VAL_DIM].astype(F32)
            ret_s[rows, vcols] = (y * (rg * jax.nn.sigmoid(rg))).astype(BF16)

    a = _dot_tn(att_t[...], wa_ref[...])
    r = _dot(ret_s[...], wr_ref[...])
    ga = p_ref[:, OFF_GA:OFF_GA + D_MODEL].astype(F32)
    gr = p_ref[:, OFF_GR:OFF_GR + D_MODEL].astype(F32)
    merged = (jax.nn.sigmoid(ga) * a + jax.nn.sigmoid(gr) * r).astype(BF16)
    o_ref[...] = x_ref[...] + _dot(merged, wo_ref[...])


def _mixer(proj, x2, consts, sink_row, gain, wa, wr, wo, batch, seq):
    bias, dintra, dq, dk, dc = consts
    nblk = seq // MIX_BLK
    cpb = MIX_BLK // CHUNK
    kv_w = 2 * ATT_KV_W
    assert OFF_AK % kv_w == 0
    row = lambda b, j: (b * nblk + j, 0)
    halo = lambda b, j: (jnp.maximum((b * nblk + j) * cpb - 1, 0), OFF_AK // kv_w)
    const2 = lambda b, j: (0, 0)
    const3 = lambda b, j: (0, 0, 0)
    resident = functools.partial(pl.BlockSpec, pipeline_mode=pl.Buffered(1))
    return pl.pallas_call(
        _mixer_kernel,
        grid=(batch, nblk),
        in_specs=[
            pl.BlockSpec((MIX_BLK, D_IN), row),
            pl.BlockSpec((CHUNK, kv_w), halo),
            pl.BlockSpec((MIX_BLK, D_MODEL), row),
            resident(bias.shape, const3),
            resident(sink_row.shape, const3),
            resident(dintra.shape, const3),
            resident(dq.shape, const3),
            resident(dk.shape, const3),
            pl.BlockSpec(memory_space=pltpu.SMEM),
            resident(gain.shape, const2),
            resident(wa.shape, const2),
            resident(wr.shape, const2),
            resident(wo.shape, const2),
        ],
        out_specs=pl.BlockSpec((MIX_BLK, D_MODEL), row),
        out_shape=jax.ShapeDtypeStruct(x2.shape, F32),
        scratch_shapes=[
            pltpu.VMEM((RET_HEADS, RET_KEY_DIM, RET_VAL_DIM), F32),
            pltpu.VMEM((ATT_Q_W, MIX_BLK), BF16),
            pltpu.VMEM((MIX_BLK, RET_V_W), BF16),
        ],
        compiler_params=pltpu.CompilerParams(
            dimension_semantics=("arbitrary", "arbitrary"), vmem_limit_bytes=VMEM_LIMIT),
        name="mixer",
    )(proj, proj, x2, bias, sink_row, dintra, dq, dk, dc, gain, wa, wr, wo)


def _mixer_constants():
    f32 = np.float32
    i = np.arange(CHUNK)[None, :]
    jj = np.arange(2 * CHUNK)[:, None]
    dist = i + CHUNK - jj
    valid = (dist >= 0) & (dist < WINDOW)
    slopes = np.exp2(f32(-8.0) * (np.arange(ATT_HEADS, dtype=f32) + f32(1.0)) / f32(ATT_HEADS))

    def table(ok):
        b = np.where(ok[None], -slopes[:, None, None] * dist.astype(f32)[None], f32(NEG))
        b = b.reshape(ATT_KV_HEADS, ATT_GROUP, 2 * CHUNK, CHUNK).transpose(0, 2, 1, 3)
        return b.reshape(ATT_KV_HEADS, 2 * CHUNK, ATT_GROUP * CHUNK)

    bias = np.concatenate([table(valid), table(valid & (jj >= CHUNK))], axis=0)

    log_g = np.log(f32(1.0) - np.exp2(f32(-5.0) - np.arange(RET_HEADS, dtype=f32)))
    idx = np.arange(CHUNK, dtype=f32)
    diff = idx[:, None] - idx[None, :]
    scale = f32(RET_KEY_DIM ** -0.5)
    dintra = np.where(diff >= 0, np.exp(log_g[:, None, None] * np.maximum(diff, f32(0.0))), f32(0.0)) * scale
    dq = np.exp(log_g[:, None] * (idx + f32(1.0)))
    dk = np.exp(log_g[:, None] * (f32(CHUNK - 1.0) - idx)) * scale
    dq = np.broadcast_to(dq[:, :, None], (RET_HEADS, CHUNK, RET_KEY_DIM))
    dk = np.broadcast_to(dk[:, :, None], (RET_HEADS, CHUNK, RET_KEY_DIM))
    dc = np.exp(log_g * f32(CHUNK))
    return tuple(jnp.asarray(np.ascontiguousarray(t), dtype=F32) for t in (bias, dintra, dq, dk, dc))


def _ffn_kernel(x_ref, g_ref, wg_ref, wu_ref, wd_ref, fg_ref, o_ref, *, final):
    x = x_ref[...]
    h = _rms(x, g_ref[...]).astype(BF16)
    acc = x
    for c in range(D_FF // FFN_CH):
        cols = slice(c * FFN_CH, (c + 1) * FFN_CH)
        gate = _dot(h, wg_ref[:, cols].astype(BF16))
        up = _dot(h, wu_ref[:, cols].astype(BF16))
        act = (gate * jax.nn.sigmoid(gate) * up).astype(BF16)
        acc = acc + _dot(act, wd_ref[cols, :].astype(BF16))
    if final:
        acc = _rms(acc, fg_ref[...])
    o_ref[...] = acc


def _ffn(x2, g, wg, wu, wd, fg, layer, final):
    t = x2.shape[0]
    const = lambda i: (0, 0)

    def resident(shape, _):
        return pl.BlockSpec((None,) + tuple(shape[1:]), lambda i: (layer, 0, 0), pipeline_mode=pl.Buffered(1))

    return pl.pallas_call(
        functools.partial(_ffn_kernel, final=final),
        grid=(t // FFN_BM,),
        in_specs=[
            pl.BlockSpec((FFN_BM, D_MODEL), lambda i: (i, 0)),
            pl.BlockSpec((1, D_MODEL), const),
            resident(wg.shape, const),
            resident(wu.shape, const),
            resident(wd.shape, const),
            pl.BlockSpec((1, D_MODEL), const),
        ],
        out_specs=pl.BlockSpec((FFN_BM, D_MODEL), lambda i: (i, 0)),
        out_shape=jax.ShapeDtypeStruct(x2.shape, F32),
        compiler_params=pltpu.CompilerParams(
            dimension_semantics=("arbitrary",), vmem_limit_bytes=VMEM_LIMIT),
        name="ffn_final" if final else "ffn",
    )(x2, g, wg, wu, wd, fg)


def kernel(x, norm_mix, w_in, att_sinks, ret_gn_gain, w_att_o, w_ret_o, w_out, norm_ffn, w_gate, w_up, w_down,
           final_norm):
    batch, seq, d = x.shape
    depth = w_in.shape[0]
    assert d == D_MODEL and seq % MIX_BLK == 0 and (batch * seq) % PROJ_BM == 0 and (batch * seq) % FFN_BM == 0
    x2 = x.reshape(batch * seq, d)
    consts = _mixer_constants()
    fg = final_norm.reshape(1, d)
    for l in range(depth):
        proj = _inproj(x2, norm_mix[l].reshape(1, d), w_in, l)
        sink_row = jnp.repeat(att_sinks[l].astype(F32).reshape(ATT_KV_HEADS, ATT_GROUP), CHUNK, axis=1)[:, None, :]
        x2 = _mixer(proj, x2, consts, sink_row, ret_gn_gain[l].reshape(1, RET_V_W),
                    w_att_o[l].astype(BF16), w_ret_o[l].astype(BF16), w_out[l].astype(BF16), batch, seq)
        x2 = _ffn(x2, norm_ffn[l].reshape(1, d), w_gate, w_up, w_down, fg, l, final=(l == depth - 1))
    return x2.reshape(batch, seq, d)
```

```python
import functools

import jax
import jax.numpy as jnp
import numpy as np
from jax import lax
from jax.experimental import pallas as pl
from jax.experimental.pallas import tpu as pltpu

F32 = jnp.float32
BF16 = jnp.bfloat16

D_MODEL = 1024
ATT_HEADS = 8
ATT_KV_HEADS = 2
ATT_HEAD_DIM = 64
ATT_GROUP = ATT_HEADS // ATT_KV_HEADS
WINDOW = 128
CHUNK = 128
RET_HEADS = 4
RET_KEY_DIM = 128
RET_VAL_DIM = 256
D_FF = 2816
EPS = 1e-6

ATT_Q_W = ATT_HEADS * ATT_HEAD_DIM
ATT_KV_W = ATT_KV_HEADS * ATT_HEAD_DIM
RET_QK_W = RET_HEADS * RET_KEY_DIM
RET_V_W = RET_HEADS * RET_VAL_DIM
OFF_AQ = 0
OFF_AK = OFF_AQ + ATT_Q_W
OFF_AV = OFF_AK + ATT_KV_W
OFF_RQ = OFF_AV + ATT_KV_W
OFF_RK = OFF_RQ + RET_QK_W
OFF_RV = OFF_RK + RET_QK_W
OFF_RG = OFF_RV + RET_V_W
OFF_GA = OFF_RG + RET_V_W
OFF_GR = OFF_GA + D_MODEL
D_IN = OFF_GR + D_MODEL

NEG = -1e30
SUM_ROWS = 8

V7X_VMEM_BYTES = 64 * 1024 * 1024
VMEM_LIMIT = V7X_VMEM_BYTES - 8 * 1024 * 1024

PROJ_BM = 512
PROJ_NCH = 256
MIX_BLK = 512
FFN_BM = 1024
FFN_CH = 256


def _rms(x, g):
    return x * lax.rsqrt(jnp.mean(x * x, axis=-1, keepdims=True) + EPS) * g


def _dot(a, b):
    return jnp.dot(a, b, preferred_element_type=F32)


def _dot_nt(a, b):
    return lax.dot_general(a, b, (((1,), (1,)), ((), ())), preferred_element_type=F32)


def _dot_tn(a, b):
    return lax.dot_general(a, b, (((0,), (0,)), ((), ())), preferred_element_type=F32)


def _inproj_kernel(x_ref, g_ref, w_ref, o_ref):
    h = _rms(x_ref[...], g_ref[...]).astype(BF16)
    for c in range(D_IN // PROJ_NCH):
        cols = slice(c * PROJ_NCH, (c + 1) * PROJ_NCH)
        o_ref[:, cols] = _dot(h, w_ref[:, cols].astype(BF16)).astype(BF16)


def _inproj(x2, g, w, layer):
    t = x2.shape[0]
    return pl.pallas_call(
        _inproj_kernel,
        grid=(t // PROJ_BM,),
        in_specs=[
            pl.BlockSpec((PROJ_BM, D_MODEL), lambda i: (i, 0)),
            pl.BlockSpec((1, D_MODEL), lambda i: (0, 0)),
            pl.BlockSpec((None, D_MODEL, D_IN), lambda i: (layer, 0, 0), pipeline_mode=pl.Buffered(1)),
        ],
        out_specs=pl.BlockSpec((PROJ_BM, D_IN), lambda i: (i, 0)),
        out_shape=jax.ShapeDtypeStruct((t, D_IN), BF16),
        compiler_params=pltpu.CompilerParams(
            dimension_semantics=("arbitrary",), vmem_limit_bytes=VMEM_LIMIT),
        name="inproj",
    )(x2, g, w)


def _mixer_kernel(p_ref, kvp_ref, x_ref, bias_ref, sink_ref, dintra_ref, dq_ref, dk_ref, dc_ref,
                  gain_ref, wa_ref, wr_ref, wo_ref, o_ref,
                  state, att_t, ret_s):
    j = pl.program_id(1)

    @pl.when(j == 0)
    def _():
        state[...] = jnp.zeros_like(state)

    first = jnp.where(j == 0, 1, 0)

    for c in range(MIX_BLK // CHUNK):
        rows = slice(c * CHUNK, (c + 1) * CHUNK)
        if c == 0:
            kv_p = kvp_ref[...]
        else:
            kv_p = p_ref[(c - 1) * CHUNK:c * CHUNK, OFF_AK:OFF_AK + 2 * ATT_KV_W]
        kv_c = p_ref[rows, OFF_AK:OFF_AK + 2 * ATT_KV_W]
        kv = jnp.concatenate([kv_p, kv_c], axis=0)
        q = p_ref[rows, OFF_AQ:OFF_AQ + ATT_Q_W]
        for kh in range(ATT_KV_HEADS):
            qs = jnp.concatenate(
                [q[:, (kh * ATT_GROUP + g) * ATT_HEAD_DIM:(kh * ATT_GROUP + g + 1) * ATT_HEAD_DIM]
                 for g in range(ATT_GROUP)], axis=0)
            kk = kv[:, kh * ATT_HEAD_DIM:(kh + 1) * ATT_HEAD_DIM] * (ATT_HEAD_DIM ** -0.5)
            vv = kv[:, ATT_KV_W + kh * ATT_HEAD_DIM:ATT_KV_W + (kh + 1) * ATT_HEAD_DIM]
            bias = bias_ref[first * ATT_KV_HEADS + kh] if c == 0 else bias_ref[kh]
            s = _dot_nt(kk, qs) + bias
            sink = sink_ref[kh]
            m = jnp.maximum(jnp.max(s, axis=0, keepdims=True), sink)
            p = jnp.exp((s - m).astype(BF16))
            ov = _dot_tn(jnp.concatenate([vv, jnp.ones((2 * CHUNK, SUM_ROWS), BF16)], axis=1), p)
            denom = ov[ATT_HEAD_DIM:ATT_HEAD_DIM + 1] + jnp.exp(sink - m)
            o = ov[:ATT_HEAD_DIM] / denom
            for g in range(ATT_GROUP):
                h = kh * ATT_GROUP + g
                att_t[h * ATT_HEAD_DIM:(h + 1) * ATT_HEAD_DIM, rows] = o[:, g * CHUNK:(g + 1) * CHUNK].astype(BF16)

        for h in range(RET_HEADS):
            rq = p_ref[rows, OFF_RQ + h * RET_KEY_DIM:OFF_RQ + (h + 1) * RET_KEY_DIM]
            rk = p_ref[rows, OFF_RK + h * RET_KEY_DIM:OFF_RK + (h + 1) * RET_KEY_DIM]
            rv = p_ref[rows, OFF_RV + h * RET_VAL_DIM:OFF_RV + (h + 1) * RET_VAL_DIM]
            st = state[h]
            sc = (_dot_nt(rq, rk) * dintra_ref[h]).astype(BF16)
            qd = (rq.astype(F32) * dq_ref[h]).astype(BF16)
            kd = (rk.astype(F32) * dk_ref[h]).astype(BF16)
            lhs = jnp.concatenate([jnp.concatenate([sc, qd], axis=1),
                                   jnp.concatenate([kd.T, jnp.zeros_like(kd)], axis=1)], axis=0)
            res = _dot(lhs, jnp.concatenate([rv, st.astype(BF16)], axis=0))
            o = res[:CHUNK]
            state[h] = dc_ref[h] * st + res[CHUNK:]
            mu = jnp.mean(o, axis=-1, keepdims=True)
            oc = o - mu
            var = jnp.mean(oc * oc, axis=-1, keepdims=True)
            vcols = slice(h * RET_VAL_DIM, (h + 1) * RET_VAL_DIM)
            y = oc * lax.rsqrt(var + EPS) * gain_ref[:, vcols]
            rg = p_ref[rows, OFF_RG + h * RET_VAL_DIM:OFF_RG + (h + 1) * RET_VAL_DIM].astype(F32)
            ret_s[rows, vcols] = (y * (rg * jax.nn.sigmoid(rg))).astype(BF16)

    a = _dot_tn(att_t[...], wa_ref[...].astype(BF16))
    r = _dot(ret_s[...], wr_ref[...].astype(BF16))
    ga = p_ref[:, OFF_GA:OFF_GA + D_MODEL].astype(F32)
    gr = p_ref[:, OFF_GR:OFF_GR + D_MODEL].astype(F32)
    merged = (jax.nn.sigmoid(ga) * a + jax.nn.sigmoid(gr) * r).astype(BF16)
    o_ref[...] = x_ref[...] + _dot(merged, wo_ref[...].astype(BF16))


def _mixer(proj, x2, consts, sink_row, gain, wa, wr, wo, layer, batch, seq):
    bias, dintra, dq, dk, dc = consts
    nblk = seq // MIX_BLK
    cpb = MIX_BLK // CHUNK
    kv_w = 2 * ATT_KV_W
    assert OFF_AK % kv_w == 0
    row = lambda b, j: (b * nblk + j, 0)
    halo = lambda b, j: (jnp.maximum((b * nblk + j) * cpb - 1, 0), OFF_AK // kv_w)
    const2 = lambda b, j: (0, 0)
    const3 = lambda b, j: (0, 0, 0)
    of_layer = lambda b, j: (layer, 0, 0)
    resident = functools.partial(pl.BlockSpec, pipeline_mode=pl.Buffered(1))
    return pl.pallas_call(
        _mixer_kernel,
        grid=(batch, nblk),
        in_specs=[
            pl.BlockSpec((MIX_BLK, D_IN), row),
            pl.BlockSpec((CHUNK, kv_w), halo),
            pl.BlockSpec((MIX_BLK, D_MODEL), row),
            resident(bias.shape, const3),
            resident(sink_row.shape, const3),
            resident(dintra.shape, const3),
            resident(dq.shape, const3),
            resident(dk.shape, const3),
            pl.BlockSpec(memory_space=pltpu.SMEM),
            resident(gain.shape, const2),
            resident((None,) + wa.shape[1:], of_layer),
            resident((None,) + wr.shape[1:], of_layer),
            resident((None,) + wo.shape[1:], of_layer),
        ],
        out_specs=pl.BlockSpec((MIX_BLK, D_MODEL), row),
        out_shape=jax.ShapeDtypeStruct(x2.shape, F32),
        scratch_shapes=[
            pltpu.VMEM((RET_HEADS, RET_KEY_DIM, RET_VAL_DIM), F32),
            pltpu.VMEM((ATT_Q_W, MIX_BLK), BF16),
            pltpu.VMEM((MIX_BLK, RET_V_W), BF16),
        ],
        compiler_params=pltpu.CompilerParams(
            dimension_semantics=("arbitrary", "arbitrary"), vmem_limit_bytes=VMEM_LIMIT),
        name="mixer",
    )(proj, proj, x2, bias, sink_row, dintra, dq, dk, dc, gain, wa, wr, wo)


def _mixer_constants():
    f32 = np.float32
    i = np.arange(CHUNK)[None, :]
    jj = np.arange(2 * CHUNK)[:, None]
    dist = i + CHUNK - jj
    valid = (dist >= 0) & (dist < WINDOW)
    slopes = np.exp2(f32(-8.0) * (np.arange(ATT_HEADS, dtype=f32) + f32(1.0)) / f32(ATT_HEADS))

    def table(ok):
        b = np.where(ok[None], -slopes[:, None, None] * dist.astype(f32)[None], f32(NEG))
        b = b.reshape(ATT_KV_HEADS, ATT_GROUP, 2 * CHUNK, CHUNK).transpose(0, 2, 1, 3)
        return b.reshape(ATT_KV_HEADS, 2 * CHUNK, ATT_GROUP * CHUNK)

    bias = np.concatenate([table(valid), table(valid & (jj >= CHUNK))], axis=0)

    log_g = np.log(f32(1.0) - np.exp2(f32(-5.0) - np.arange(RET_HEADS, dtype=f32)))
    idx = np.arange(CHUNK, dtype=f32)
    diff = idx[:, None] - idx[None, :]
    scale = f32(RET_KEY_DIM ** -0.5)
    dintra = np.where(diff >= 0, np.exp(log_g[:, None, None] * np.maximum(diff, f32(0.0))), f32(0.0)) * scale
    dq = np.exp(log_g[:, None] * (idx + f32(1.0)))
    dk = np.exp(log_g[:, None] * (f32(CHUNK - 1.0) - idx)) * scale
    dq = np.broadcast_to(dq[:, :, None], (RET_HEADS, CHUNK, RET_KEY_DIM))
    dk = np.broadcast_to(dk[:, :, None], (RET_HEADS, CHUNK, RET_KEY_DIM))
    dc = np.exp(log_g * f32(CHUNK))
    return tuple(jnp.asarray(np.ascontiguousarray(t), dtype=F32) for t in (bias, dintra, dq, dk, dc))


def _ffn_kernel(x_ref, g_ref, wg_ref, wu_ref, wd_ref, fg_ref, o_ref, *, final):
    x = x_ref[...]
    h = _rms(x, g_ref[...]).astype(BF16)
    acc = x
    for c in range(D_FF // FFN_CH):
        cols = slice(c * FFN_CH, (c + 1) * FFN_CH)
        gate = _dot(h, wg_ref[:, cols].astype(BF16))
        up = _dot(h, wu_ref[:, cols].astype(BF16))
        act = (gate * jax.nn.sigmoid(gate) * up).astype(BF16)
        acc = acc + _dot(act, wd_ref[cols, :].astype(BF16))
    if final:
        acc = _rms(acc, fg_ref[...])
    o_ref[...] = acc


def _ffn(x2, g, wg, wu, wd, fg, layer, final):
    t = x2.shape[0]
    const = lambda i: (0, 0)

    def resident(shape, _):
        return pl.BlockSpec((None,) + tuple(shape[1:]), lambda i: (layer, 0, 0), pipeline_mode=pl.Buffered(1))

    return pl.pallas_call(
        functools.partial(_ffn_kernel, final=final),
        grid=(t // FFN_BM,),
        in_specs=[
            pl.BlockSpec((FFN_BM, D_MODEL), lambda i: (i, 0)),
            pl.BlockSpec((1, D_MODEL), const),
            resident(wg.shape, const),
            resident(wu.shape, const),
            resident(wd.shape, const),
            pl.BlockSpec((1, D_MODEL), const),
        ],
        out_specs=pl.BlockSpec((FFN_BM, D_MODEL), lambda i: (i, 0)),
        out_shape=jax.ShapeDtypeStruct(x2.shape, F32),
        compiler_params=pltpu.CompilerParams(
            dimension_semantics=("arbitrary",), vmem_limit_bytes=VMEM_LIMIT),
        name="ffn_final" if final else "ffn",
    )(x2, g, wg, wu, wd, fg)


def kernel(x, norm_mix, w_in, att_sinks, ret_gn_gain, w_att_o, w_ret_o, w_out, norm_ffn, w_gate, w_up, w_down,
           final_norm):
    batch, seq, d = x.shape
    depth = w_in.shape[0]
    assert d == D_MODEL and seq % MIX_BLK == 0 and (batch * seq) % PROJ_BM == 0 and (batch * seq) % FFN_BM == 0
    x2 = x.reshape(batch * seq, d)
    consts = _mixer_constants()
    fg = final_norm.reshape(1, d)
    for l in range(depth):
        proj = _inproj(x2, norm_mix[l].reshape(1, d), w_in, l)
        sink_row = jnp.repeat(att_sinks[l].astype(F32).reshape(ATT_KV_HEADS, ATT_GROUP), CHUNK, axis=1)[:, None, :]
        x2 = _mixer(proj, x2, consts, sink_row, ret_gn_gain[l].reshape(1, RET_V_W),
                    w_att_o, w_ret_o, w_out, l, batch, seq)
        x2 = _ffn(x2, norm_ffn[l].reshape(1, d), w_gate, w_up, w_down, fg, l, final=(l == depth - 1))
    return x2.reshape(batch, seq, d)
```

```python
import functools

import jax
import jax.numpy as jnp
import numpy as np
from jax import lax
from jax.experimental import pallas as pl
from jax.experimental.pallas import tpu as pltpu

F32 = jnp.float32
BF16 = jnp.bfloat16

D_MODEL = 1024
ATT_HEADS = 8
ATT_KV_HEADS = 2
ATT_HEAD_DIM = 64
ATT_GROUP = ATT_HEADS // ATT_KV_HEADS
WINDOW = 128
CHUNK = 128
RET_HEADS = 4
RET_KEY_DIM = 128
RET_VAL_DIM = 256
D_FF = 2816
EPS = 1e-6

ATT_Q_W = ATT_HEADS * ATT_HEAD_DIM
ATT_KV_W = ATT_KV_HEADS * ATT_HEAD_DIM
RET_QK_W = RET_HEADS * RET_KEY_DIM
RET_V_W = RET_HEADS * RET_VAL_DIM
OFF_AQ = 0
OFF_AK = OFF_AQ + ATT_Q_W
OFF_AV = OFF_AK + ATT_KV_W
OFF_RQ = OFF_AV + ATT_KV_W
OFF_RK = OFF_RQ + RET_QK_W
OFF_RV = OFF_RK + RET_QK_W
OFF_RG = OFF_RV + RET_V_W
OFF_GA = OFF_RG + RET_V_W
OFF_GR = OFF_GA + D_MODEL
D_IN = OFF_GR + D_MODEL

NEG = -1e30
SUM_ROWS = 8

V7X_VMEM_BYTES = 64 * 1024 * 1024
VMEM_LIMIT = V7X_VMEM_BYTES - 8 * 1024 * 1024

PROJ_BM = 512
PROJ_NCH = 256
MIX_BLK = 512
FFN_BM = 1024
FFN_CH = 256
FFN_STAGE_SLOTS = 2


def _rms(x, g):
    return x * lax.rsqrt(jnp.mean(x * x, axis=-1, keepdims=True) + EPS) * g


def _dot(a, b):
    return jnp.dot(a, b, preferred_element_type=F32)


def _dot_nt(a, b):
    return lax.dot_general(a, b, (((1,), (1,)), ((), ())), preferred_element_type=F32)


def _dot_tn(a, b):
    return lax.dot_general(a, b, (((0,), (0,)), ((), ())), preferred_element_type=F32)


def _inproj_kernel(x_ref, g_ref, w_ref, o_ref):
    h = _rms(x_ref[...], g_ref[...]).astype(BF16)
    for c in range(D_IN // PROJ_NCH):
        cols = slice(c * PROJ_NCH, (c + 1) * PROJ_NCH)
        o_ref[:, cols] = _dot(h, w_ref[:, cols].astype(BF16)).astype(BF16)


def _inproj(x2, g, w, layer):
    t = x2.shape[0]
    return pl.pallas_call(
        _inproj_kernel,
        grid=(t // PROJ_BM,),
        in_specs=[
            pl.BlockSpec((PROJ_BM, D_MODEL), lambda i: (i, 0)),
            pl.BlockSpec((1, D_MODEL), lambda i: (0, 0)),
            pl.BlockSpec((None, D_MODEL, D_IN), lambda i: (layer, 0, 0), pipeline_mode=pl.Buffered(1)),
        ],
        out_specs=pl.BlockSpec((PROJ_BM, D_IN), lambda i: (i, 0)),
        out_shape=jax.ShapeDtypeStruct((t, D_IN), BF16),
        compiler_params=pltpu.CompilerParams(
            dimension_semantics=("arbitrary",), vmem_limit_bytes=VMEM_LIMIT),
        name="inproj",
    )(x2, g, w)


def _mixer_kernel(p_ref, kvp_ref, x_ref, bias_ref, sink_ref, dintra_ref, dq_ref, dk_ref, dc_ref,
                  gain_ref, wa_ref, wr_ref, wo_ref, o_ref,
                  state, att_t, ret_s):
    j = pl.program_id(1)

    @pl.when(j == 0)
    def _():
        state[...] = jnp.zeros_like(state)

    first = jnp.where(j == 0, 1, 0)

    for c in range(MIX_BLK // CHUNK):
        rows = slice(c * CHUNK, (c + 1) * CHUNK)
        if c == 0:
            kv_p = kvp_ref[...]
        else:
            kv_p = p_ref[(c - 1) * CHUNK:c * CHUNK, OFF_AK:OFF_AK + 2 * ATT_KV_W]
        kv_c = p_ref[rows, OFF_AK:OFF_AK + 2 * ATT_KV_W]
        kv = jnp.concatenate([kv_p, kv_c], axis=0)
        q = p_ref[rows, OFF_AQ:OFF_AQ + ATT_Q_W]
        for kh in range(ATT_KV_HEADS):
            qs = jnp.concatenate(
                [q[:, (kh * ATT_GROUP + g) * ATT_HEAD_DIM:(kh * ATT_GROUP + g + 1) * ATT_HEAD_DIM]
                 for g in range(ATT_GROUP)], axis=0)
            kk = kv[:, kh * ATT_HEAD_DIM:(kh + 1) * ATT_HEAD_DIM] * (ATT_HEAD_DIM ** -0.5)
            vv = kv[:, ATT_KV_W + kh * ATT_HEAD_DIM:ATT_KV_W + (kh + 1) * ATT_HEAD_DIM]
            bias = bias_ref[first * ATT_KV_HEADS + kh] if c == 0 else bias_ref[kh]
            s = _dot_nt(kk, qs) + bias
            sink = sink_ref[kh]
            m = jnp.maximum(jnp.max(s, axis=0, keepdims=True), sink)
            p = jnp.exp((s - m).astype(BF16))
            ov = _dot_tn(jnp.concatenate([vv, jnp.ones((2 * CHUNK, SUM_ROWS), BF16)], axis=1), p)
            denom = ov[ATT_HEAD_DIM:ATT_HEAD_DIM + 1] + jnp.exp(sink - m)
            o = ov[:ATT_HEAD_DIM] / denom
            for g in range(ATT_GROUP):
                h = kh * ATT_GROUP + g
                att_t[h * ATT_HEAD_DIM:(h + 1) * ATT_HEAD_DIM, rows] = o[:, g * CHUNK:(g + 1) * CHUNK].astype(BF16)

        for h in range(RET_HEADS):
            rq = p_ref[rows, OFF_RQ + h * RET_KEY_DIM:OFF_RQ + (h + 1) * RET_KEY_DIM]
            rk = p_ref[rows, OFF_RK + h * RET_KEY_DIM:OFF_RK + (h + 1) * RET_KEY_DIM]
            rv = p_ref[rows, OFF_RV + h * RET_VAL_DIM:OFF_RV + (h + 1) * RET_VAL_DIM]
            st = state[h]
            sc = (_dot_nt(rq, rk) * dintra_ref[h]).astype(BF16)
            qd = (rq.astype(F32) * dq_ref[h]).astype(BF16)
            kd = (rk.astype(F32) * dk_ref[h]).astype(BF16)
            lhs = jnp.concatenate([jnp.concatenate([sc, qd], axis=1),
                                   jnp.concatenate([kd.T, jnp.zeros_like(kd)], axis=1)], axis=0)
            res = _dot(lhs, jnp.concatenate([rv, st.astype(BF16)], axis=0))
            o = res[:CHUNK]
            state[h] = dc_ref[h] * st + res[CHUNK:]
            mu = jnp.mean(o, axis=-1, keepdims=True)
            oc = o - mu
            var = jnp.mean(oc * oc, axis=-1, keepdims=True)
            vcols = slice(h * RET_VAL_DIM, (h + 1) * RET_VAL_DIM)
            y = oc * lax.rsqrt(var + EPS) * gain_ref[:, vcols]
            rg = p_ref[rows, OFF_RG + h * RET_VAL_DIM:OFF_RG + (h + 1) * RET_VAL_DIM].astype(F32)
            ret_s[rows, vcols] = (y * (rg * jax.nn.sigmoid(rg))).astype(BF16)

    a = _dot_tn(att_t[...], wa_ref[...])
    r = _dot(ret_s[...], wr_ref[...])
    ga = p_ref[:, OFF_GA:OFF_GA + D_MODEL].astype(F32)
    gr = p_ref[:, OFF_GR:OFF_GR + D_MODEL].astype(F32)
    merged = (jax.nn.sigmoid(ga) * a + jax.nn.sigmoid(gr) * r).astype(BF16)
    o_ref[...] = x_ref[...] + _dot(merged, wo_ref[...])


def _mixer(proj, x2, consts, sink_row, gain, wa, wr, wo, batch, seq):
    bias, dintra, dq, dk, dc = consts
    nblk = seq // MIX_BLK
    cpb = MIX_BLK // CHUNK
    kv_w = 2 * ATT_KV_W
    assert OFF_AK % kv_w == 0
    row = lambda b, j: (b * nblk + j, 0)
    halo = lambda b, j: (jnp.maximum((b * nblk + j) * cpb - 1, 0), OFF_AK // kv_w)
    const2 = lambda b, j: (0, 0)
    const3 = lambda b, j: (0, 0, 0)
    resident = functools.partial(pl.BlockSpec, pipeline_mode=pl.Buffered(1))
    return pl.pallas_call(
        _mixer_kernel,
        grid=(batch, nblk),
        in_specs=[
            pl.BlockSpec((MIX_BLK, D_IN), row),
            pl.BlockSpec((CHUNK, kv_w), halo),
            pl.BlockSpec((MIX_BLK, D_MODEL), row),
            resident(bias.shape, const3),
            resident(sink_row.shape, const3),
            resident(dintra.shape, const3),
            resident(dq.shape, const3),
            resident(dk.shape, const3),
            pl.BlockSpec(memory_space=pltpu.SMEM),
            resident(gain.shape, const2),
            resident(wa.shape, const2),
            resident(wr.shape, const2),
            resident(wo.shape, const2),
        ],
        out_specs=pl.BlockSpec((MIX_BLK, D_MODEL), row),
        out_shape=jax.ShapeDtypeStruct(x2.shape, F32),
        scratch_shapes=[
            pltpu.VMEM((RET_HEADS, RET_KEY_DIM, RET_VAL_DIM), F32),
            pltpu.VMEM((ATT_Q_W, MIX_BLK), BF16),
            pltpu.VMEM((MIX_BLK, RET_V_W), BF16),
        ],
        compiler_params=pltpu.CompilerParams(
            dimension_semantics=("arbitrary", "arbitrary"), vmem_limit_bytes=VMEM_LIMIT),
        name="mixer",
    )(proj, proj, x2, bias, sink_row, dintra, dq, dk, dc, gain, wa, wr, wo)


def _mixer_constants():
    f32 = np.float32
    i = np.arange(CHUNK)[None, :]
    jj = np.arange(2 * CHUNK)[:, None]
    dist = i + CHUNK - jj
    valid = (dist >= 0) & (dist < WINDOW)
    slopes = np.exp2(f32(-8.0) * (np.arange(ATT_HEADS, dtype=f32) + f32(1.0)) / f32(ATT_HEADS))

    def table(ok):
        b = np.where(ok[None], -slopes[:, None, None] * dist.astype(f32)[None], f32(NEG))
        b = b.reshape(ATT_KV_HEADS, ATT_GROUP, 2 * CHUNK, CHUNK).transpose(0, 2, 1, 3)
        return b.reshape(ATT_KV_HEADS, 2 * CHUNK, ATT_GROUP * CHUNK)

    bias = np.concatenate([table(valid), table(valid & (jj >= CHUNK))], axis=0)

    log_g = np.log(f32(1.0) - np.exp2(f32(-5.0) - np.arange(RET_HEADS, dtype=f32)))
    idx = np.arange(CHUNK, dtype=f32)
    diff = idx[:, None] - idx[None, :]
    scale = f32(RET_KEY_DIM ** -0.5)
    dintra = np.where(diff >= 0, np.exp(log_g[:, None, None] * np.maximum(diff, f32(0.0))), f32(0.0)) * scale
    dq = np.exp(log_g[:, None] * (idx + f32(1.0)))
    dk = np.exp(log_g[:, None] * (f32(CHUNK - 1.0) - idx)) * scale
    dq = np.broadcast_to(dq[:, :, None], (RET_HEADS, CHUNK, RET_KEY_DIM))
    dk = np.broadcast_to(dk[:, :, None], (RET_HEADS, CHUNK, RET_KEY_DIM))
    dc = np.exp(log_g * f32(CHUNK))
    return tuple(jnp.asarray(np.ascontiguousarray(t), dtype=F32) for t in (bias, dintra, dq, dk, dc))


def _ffn_weight_copies(w_hbm, stage, sem, layer, c):
    slot = c % FFN_STAGE_SLOTS
    chunk = pl.ds(c * FFN_CH, FFN_CH)
    srcs = (w_hbm[0].at[layer, :, chunk], w_hbm[1].at[layer, :, chunk], w_hbm[2].at[layer, chunk, :])
    return [pltpu.make_async_copy(src, stage[k].at[slot], sem.at[k, slot]) for k, src in enumerate(srcs)]


def _ffn_kernel(x_ref, g_ref, wg_hbm, wu_hbm, wd_hbm, fg_ref, o_ref,
                wg_s, wu_s, wd_s, sg, su, sd, sem, *, final, layer):
    nch = D_FF // FFN_CH
    copies = functools.partial(_ffn_weight_copies, (wg_hbm, wu_hbm, wd_hbm), (sg, su, sd), sem, layer)

    def body(stage_weights):
        if stage_weights:
            for c in range(min(FFN_STAGE_SLOTS, nch)):
                for cp in copies(c):
                    cp.start()
        x = x_ref[...]
        h = _rms(x, g_ref[...]).astype(BF16)
        acc = x
        for c in range(nch):
            cols = slice(c * FFN_CH, (c + 1) * FFN_CH)
            if stage_weights:
                slot = c % FFN_STAGE_SLOTS
                for cp in copies(c):
                    cp.wait()
                wg_s[:, cols] = sg[slot].astype(BF16)
                wu_s[:, cols] = su[slot].astype(BF16)
                wd_s[cols, :] = sd[slot].astype(BF16)
                if c + FFN_STAGE_SLOTS < nch:
                    for cp in copies(c + FFN_STAGE_SLOTS):
                        cp.start()
            gate = _dot(h, wg_s[:, cols])
            up = _dot(h, wu_s[:, cols])
            act = (gate * jax.nn.sigmoid(gate) * up).astype(BF16)
            acc = acc + _dot(act, wd_s[cols, :])
        if final:
            acc = _rms(acc, fg_ref[...])
        o_ref[...] = acc

    first = pl.program_id(0) == 0
    pl.when(first)(lambda: body(True))
    pl.when(jnp.logical_not(first))(lambda: body(False))


def _ffn(x2, g, wg, wu, wd, fg, layer, final):
    t = x2.shape[0]
    const = lambda i: (0, 0)
    hbm = pl.BlockSpec(memory_space=pl.ANY)
    return pl.pallas_call(
        functools.partial(_ffn_kernel, final=final, layer=layer),
        grid=(t // FFN_BM,),
        in_specs=[
            pl.BlockSpec((FFN_BM, D_MODEL), lambda i: (i, 0)),
            pl.BlockSpec((1, D_MODEL), const),
            hbm, hbm, hbm,
            pl.BlockSpec((1, D_MODEL), const),
        ],
        out_specs=pl.BlockSpec((FFN_BM, D_MODEL), lambda i: (i, 0)),
        out_shape=jax.ShapeDtypeStruct(x2.shape, F32),
        scratch_shapes=[
            pltpu.VMEM((D_MODEL, D_FF), BF16),
            pltpu.VMEM((D_MODEL, D_FF), BF16),
            pltpu.VMEM((D_FF, D_MODEL), BF16),
            pltpu.VMEM((FFN_STAGE_SLOTS, D_MODEL, FFN_CH), F32),
            pltpu.VMEM((FFN_STAGE_SLOTS, D_MODEL, FFN_CH), F32),
            pltpu.VMEM((FFN_STAGE_SLOTS, FFN_CH, D_MODEL), F32),
            pltpu.SemaphoreType.DMA((3, FFN_STAGE_SLOTS)),
        ],
        compiler_params=pltpu.CompilerParams(
            dimension_semantics=("arbitrary",), vmem_limit_bytes=VMEM_LIMIT),
        name="ffn_final" if final else "ffn",
    )(x2, g, wg, wu, wd, fg)


def kernel(x, norm_mix, w_in, att_sinks, ret_gn_gain, w_att_o, w_ret_o, w_out, norm_ffn, w_gate, w_up, w_down,
           final_norm):
    batch, seq, d = x.shape
    depth = w_in.shape[0]
    assert d == D_MODEL and seq % MIX_BLK == 0 and (batch * seq) % PROJ_BM == 0 and (batch * seq) % FFN_BM == 0
    x2 = x.reshape(batch * seq, d)
    consts = _mixer_constants()
    fg = final_norm.reshape(1, d)
    for l in range(depth):
        proj = _inproj(x2, norm_mix[l].reshape(1, d), w_in, l)
        sink_row = jnp.repeat(att_sinks[l].astype(F32).reshape(ATT_KV_HEADS, ATT_GROUP), CHUNK, axis=1)[:, None, :]
        x2 = _mixer(proj, x2, consts, sink_row, ret_gn_gain[l].reshape(1, RET_V_W),
                    w_att_o[l].astype(BF16), w_ret_o[l].astype(BF16), w_out[l].astype(BF16), batch, seq)
        x2 = _ffn(x2, norm_ffn[l].reshape(1, d), w_gate, w_up, w_down, fg, l, final=(l == depth - 1))
    return x2.reshape(batch, seq, d)
```

```python
import functools

import jax
import jax.numpy as jnp
import numpy as np
from jax import lax
from jax.experimental import pallas as pl
from jax.experimental.pallas import tpu as pltpu

F32 = jnp.float32
BF16 = jnp.bfloat16

D_MODEL = 1024
ATT_HEADS = 8
ATT_KV_HEADS = 2
ATT_HEAD_DIM = 64
ATT_GROUP = ATT_HEADS // ATT_KV_HEADS
WINDOW = 128
CHUNK = 128
RET_HEADS = 4
RET_KEY_DIM = 128
RET_VAL_DIM = 256
D_FF = 2816
EPS = 1e-6

ATT_Q_W = ATT_HEADS * ATT_HEAD_DIM
ATT_KV_W = ATT_KV_HEADS * ATT_HEAD_DIM
RET_QK_W = RET_HEADS * RET_KEY_DIM
RET_V_W = RET_HEADS * RET_VAL_DIM
OFF_AQ = 0
OFF_AK = OFF_AQ + ATT_Q_W
OFF_AV = OFF_AK + ATT_KV_W
OFF_RQ = OFF_AV + ATT_KV_W
OFF_RK = OFF_RQ + RET_QK_W
OFF_RV = OFF_RK + RET_QK_W
OFF_RG = OFF_RV + RET_V_W
OFF_GA = OFF_RG + RET_V_W
OFF_GR = OFF_GA + D_MODEL
D_IN = OFF_GR + D_MODEL

NEG = -1e30
SUM_ROWS = 8

V7X_VMEM_BYTES = 64 * 1024 * 1024
VMEM_LIMIT = V7X_VMEM_BYTES - 8 * 1024 * 1024

PROJ_BM = 512
PROJ_NCH = 256
MIX_BLK = 512
FFN_BM = 1024
FFN_CH = 256
FFN_STAGE_SLOTS = 2


def _rms(x, g):
    return x * lax.rsqrt(jnp.mean(x * x, axis=-1, keepdims=True) + EPS) * g


def _dot(a, b):
    return jnp.dot(a, b, preferred_element_type=F32)


def _dot_nt(a, b):
    return lax.dot_general(a, b, (((1,), (1,)), ((), ())), preferred_element_type=F32)


def _dot_tn(a, b):
    return lax.dot_general(a, b, (((0,), (0,)), ((), ())), preferred_element_type=F32)


def _inproj_kernel(x_ref, g_ref, w_ref, o_ref):
    h = _rms(x_ref[...], g_ref[...]).astype(BF16)
    for c in range(D_IN // PROJ_NCH):
        cols = slice(c * PROJ_NCH, (c + 1) * PROJ_NCH)
        o_ref[:, cols] = _dot(h, w_ref[:, cols].astype(BF16)).astype(BF16)


def _inproj(x2, g, w, layer):
    t = x2.shape[0]
    return pl.pallas_call(
        _inproj_kernel,
        grid=(t // PROJ_BM,),
        in_specs=[
            pl.BlockSpec((PROJ_BM, D_MODEL), lambda i: (i, 0)),
            pl.BlockSpec((1, D_MODEL), lambda i: (0, 0)),
            pl.BlockSpec((None, D_MODEL, D_IN), lambda i: (layer, 0, 0), pipeline_mode=pl.Buffered(1)),
        ],
        out_specs=pl.BlockSpec((PROJ_BM, D_IN), lambda i: (i, 0)),
        out_shape=jax.ShapeDtypeStruct((t, D_IN), BF16),
        compiler_params=pltpu.CompilerParams(
            dimension_semantics=("arbitrary",), vmem_limit_bytes=VMEM_LIMIT),
        name="inproj",
    )(x2, g, w)


def _mixer_kernel(p_ref, kvp_ref, x_ref, bias_ref, sink_ref, dintra_ref, dq_ref, dk_ref, dc_ref,
                  gain_ref, wa_ref, wr_ref, wo_ref, o_ref,
                  state, att_t, ret_s):
    j = pl.program_id(1)

    @pl.when(j == 0)
    def _():
        state[...] = jnp.zeros_like(state)

    first = jnp.where(j == 0, 1, 0)

    for c in range(MIX_BLK // CHUNK):
        rows = slice(c * CHUNK, (c + 1) * CHUNK)
        if c == 0:
            kv_p = kvp_ref[...]
        else:
            kv_p = p_ref[(c - 1) * CHUNK:c * CHUNK, OFF_AK:OFF_AK + 2 * ATT_KV_W]
        kv_c = p_ref[rows, OFF_AK:OFF_AK + 2 * ATT_KV_W]
        kv = jnp.concatenate([kv_p, kv_c], axis=0)
        q = p_ref[rows, OFF_AQ:OFF_AQ + ATT_Q_W]
        for kh in range(ATT_KV_HEADS):
            qs = jnp.concatenate(
                [q[:, (kh * ATT_GROUP + g) * ATT_HEAD_DIM:(kh * ATT_GROUP + g + 1) * ATT_HEAD_DIM]
                 for g in range(ATT_GROUP)], axis=0)
            kk = kv[:, kh * ATT_HEAD_DIM:(kh + 1) * ATT_HEAD_DIM] * (ATT_HEAD_DIM ** -0.5)
            vv = kv[:, ATT_KV_W + kh * ATT_HEAD_DIM:ATT_KV_W + (kh + 1) * ATT_HEAD_DIM]
            bias = bias_ref[first * ATT_KV_HEADS + kh] if c == 0 else bias_ref[kh]
            s = _dot_nt(kk, qs) + bias
            sink = sink_ref[kh]
            m = jnp.maximum(jnp.max(s, axis=0, keepdims=True), sink)
            p = jnp.exp((s - m).astype(BF16))
            ov = _dot_tn(jnp.concatenate([vv, jnp.ones((2 * CHUNK, SUM_ROWS), BF16)], axis=1), p)
            denom = ov[ATT_HEAD_DIM:ATT_HEAD_DIM + 1] + jnp.exp(sink - m)
            o = ov[:ATT_HEAD_DIM] / denom
            for g in range(ATT_GROUP):
                h = kh * ATT_GROUP + g
                att_t[h * ATT_HEAD_DIM:(h + 1) * ATT_HEAD_DIM, rows] = o[:, g * CHUNK:(g + 1) * CHUNK].astype(BF16)

        for h in range(RET_HEADS):
            rq = p_ref[rows, OFF_RQ + h * RET_KEY_DIM:OFF_RQ + (h + 1) * RET_KEY_DIM]
            rk = p_ref[rows, OFF_RK + h * RET_KEY_DIM:OFF_RK + (h + 1) * RET_KEY_DIM]
            rv = p_ref[rows, OFF_RV + h * RET_VAL_DIM:OFF_RV + (h + 1) * RET_VAL_DIM]
            st = state[h]
            sc = (_dot_nt(rq, rk) * dintra_ref[h]).astype(BF16)
            qd = (rq.astype(F32) * dq_ref[h]).astype(BF16)
            kd = (rk.astype(F32) * dk_ref[h]).astype(BF16)
            lhs = jnp.concatenate([jnp.concatenate([sc, qd], axis=1),
                                   jnp.concatenate([kd.T, jnp.zeros_like(kd)], axis=1)], axis=0)
            res = _dot(lhs, jnp.concatenate([rv, st.astype(BF16)], axis=0))
            o = res[:CHUNK]
            state[h] = dc_ref[h] * st + res[CHUNK:]
            mu = jnp.mean(o, axis=-1, keepdims=True)
            oc = o - mu
            var = jnp.mean(oc * oc, axis=-1, keepdims=True)
            vcols = slice(h * RET_VAL_DIM, (h + 1) * RET_VAL_DIM)
            y = oc * lax.rsqrt(var + EPS) * gain_ref[:, vcols]
            rg = p_ref[rows, OFF_RG + h * RET_VAL_DIM:OFF_RG + (h + 1) * RET_VAL_DIM].astype(F32)
            ret_s[rows, vcols] = (y * (rg * jax.nn.sigmoid(rg))).astype(BF16)

    a = _dot_tn(att_t[...], wa_ref[...])
    r = _dot(ret_s[...], wr_ref[...])
    ga = p_ref[:, OFF_GA:OFF_GA + D_MODEL].astype(F32)
    gr = p_ref[:, OFF_GR:OFF_GR + D_MODEL].astype(F32)
    merged = (jax.nn.sigmoid(ga) * a + jax.nn.sigmoid(gr) * r).astype(BF16)
    o_ref[...] = x_ref[...] + _dot(merged, wo_ref[...])


def _mixer(proj, x2, consts, sink_row, gain, wa, wr, wo, batch, seq):
    bias, dintra, dq, dk, dc = consts
    nblk = seq // MIX_BLK
    cpb = MIX_BLK // CHUNK
    kv_w = 2 * ATT_KV_W
    assert OFF_AK % kv_w == 0
    row = lambda b, j: (b * nblk + j, 0)
    halo = lambda b, j: (jnp.maximum((b * nblk + j) * cpb - 1, 0), OFF_AK // kv_w)
    const2 = lambda b, j: (0, 0)
    const3 = lambda b, j: (0, 0, 0)
    resident = functools.partial(pl.BlockSpec, pipeline_mode=pl.Buffered(1))
    return pl.pallas_call(
        _mixer_kernel,
        grid=(batch, nblk),
        in_specs=[
            pl.BlockSpec((MIX_BLK, D_IN), row),
            pl.BlockSpec((CHUNK, kv_w), halo),
            pl.BlockSpec((MIX_BLK, D_MODEL), row),
            resident(bias.shape, const3),
            resident(sink_row.shape, const3),
            resident(dintra.shape, const3),
            resident(dq.shape, const3),
            resident(dk.shape, const3),
            pl.BlockSpec(memory_space=pltpu.SMEM),
            resident(gain.shape, const2),
            resident(wa.shape, const2),
            resident(wr.shape, const2),
            resident(wo.shape, const2),
        ],
        out_specs=pl.BlockSpec((MIX_BLK, D_MODEL), row),
        out_shape=jax.ShapeDtypeStruct(x2.shape, F32),
        scratch_shapes=[
            pltpu.VMEM((RET_HEADS, RET_KEY_DIM, RET_VAL_DIM), F32),
            pltpu.VMEM((ATT_Q_W, MIX_BLK), BF16),
            pltpu.VMEM((MIX_BLK, RET_V_W), BF16),
        ],
        compiler_params=pltpu.CompilerParams(
            dimension_semantics=("arbitrary", "arbitrary"), vmem_limit_bytes=VMEM_LIMIT),
        name="mixer",
    )(proj, proj, x2, bias, sink_row, dintra, dq, dk, dc, gain, wa, wr, wo)


def _mixer_constants():
    f32 = np.float32
    i = np.arange(CHUNK)[None, :]
    jj = np.arange(2 * CHUNK)[:, None]
    dist = i + CHUNK - jj
    valid = (dist >= 0) & (dist < WINDOW)
    slopes = np.exp2(f32(-8.0) * (np.arange(ATT_HEADS, dtype=f32) + f32(1.0)) / f32(ATT_HEADS))

    def table(ok):
        b = np.where(ok[None], -slopes[:, None, None] * dist.astype(f32)[None], f32(NEG))
        b = b.reshape(ATT_KV_HEADS, ATT_GROUP, 2 * CHUNK, CHUNK).transpose(0, 2, 1, 3)
        return b.reshape(ATT_KV_HEADS, 2 * CHUNK, ATT_GROUP * CHUNK)

    bias = np.concatenate([table(valid), table(valid & (jj >= CHUNK))], axis=0)

    log_g = np.log(f32(1.0) - np.exp2(f32(-5.0) - np.arange(RET_HEADS, dtype=f32)))
    idx = np.arange(CHUNK, dtype=f32)
    diff = idx[:, None] - idx[None, :]
    scale = f32(RET_KEY_DIM ** -0.5)
    dintra = np.where(diff >= 0, np.exp(log_g[:, None, None] * np.maximum(diff, f32(0.0))), f32(0.0)) * scale
    dq = np.exp(log_g[:, None] * (idx + f32(1.0)))
    dk = np.exp(log_g[:, None] * (f32(CHUNK - 1.0) - idx)) * scale
    dq = np.broadcast_to(dq[:, :, None], (RET_HEADS, CHUNK, RET_KEY_DIM))
    dk = np.broadcast_to(dk[:, :, None], (RET_HEADS, CHUNK, RET_KEY_DIM))
    dc = np.exp(log_g * f32(CHUNK))
    return tuple(jnp.asarray(np.ascontiguousarray(t), dtype=F32) for t in (bias, dintra, dq, dk, dc))


def _ffn_weight_copies(w_hbm, stage, sem, layer, c, slot):
    chunk = pl.ds(pl.multiple_of(c * FFN_CH, FFN_CH), FFN_CH)
    srcs = (w_hbm[0].at[layer, :, chunk], w_hbm[1].at[layer, :, chunk], w_hbm[2].at[layer, chunk, :])
    return [pltpu.make_async_copy(src, stage[k].at[slot], sem.at[k, slot]) for k, src in enumerate(srcs)]


def _ffn_kernel(x_ref, g_ref, wg_hbm, wu_hbm, wd_hbm, fg_ref, o_ref,
                wg_s, wu_s, wd_s, sg, su, sd, sem, *, final, layer):
    nch = D_FF // FFN_CH
    copies = functools.partial(_ffn_weight_copies, (wg_hbm, wu_hbm, wd_hbm), (sg, su, sd), sem, layer)

    @pl.when(pl.program_id(0) == 0)
    def _():
        for cp in copies(0, 0):
            cp.start()

        def stage_chunk(c, carry):
            slot = lax.rem(c, FFN_STAGE_SLOTS)

            @pl.when(c + 1 < nch)
            def _():
                for cp in copies(c + 1, 1 - slot):
                    cp.start()

            for cp in copies(c, slot):
                cp.wait()
            wg_s[c] = sg[slot].astype(BF16)
            wu_s[c] = su[slot].astype(BF16)
            wd_s[c] = sd[slot].astype(BF16)
            return carry

        lax.fori_loop(0, nch, stage_chunk, 0)

    x = x_ref[...]
    h = _rms(x, g_ref[...]).astype(BF16)
    acc = x
    for c in range(nch):
        gate = _dot(h, wg_s[c])
        up = _dot(h, wu_s[c])
        act = (gate * jax.nn.sigmoid(gate) * up).astype(BF16)
        acc = acc + _dot(act, wd_s[c])
    if final:
        acc = _rms(acc, fg_ref[...])
    o_ref[...] = acc


def _ffn(x2, g, wg, wu, wd, fg, layer, final):
    t = x2.shape[0]
    const = lambda i: (0, 0)
    hbm = pl.BlockSpec(memory_space=pl.ANY)
    return pl.pallas_call(
        functools.partial(_ffn_kernel, final=final, layer=layer),
        grid=(t // FFN_BM,),
        in_specs=[
            pl.BlockSpec((FFN_BM, D_MODEL), lambda i: (i, 0)),
            pl.BlockSpec((1, D_MODEL), const),
            hbm, hbm, hbm,
            pl.BlockSpec((1, D_MODEL), const),
        ],
        out_specs=pl.BlockSpec((FFN_BM, D_MODEL), lambda i: (i, 0)),
        out_shape=jax.ShapeDtypeStruct(x2.shape, F32),
        scratch_shapes=[
            pltpu.VMEM((D_FF // FFN_CH, D_MODEL, FFN_CH), BF16),
            pltpu.VMEM((D_FF // FFN_CH, D_MODEL, FFN_CH), BF16),
            pltpu.VMEM((D_FF // FFN_CH, FFN_CH, D_MODEL), BF16),
            pltpu.VMEM((FFN_STAGE_SLOTS, D_MODEL, FFN_CH), F32),
            pltpu.VMEM((FFN_STAGE_SLOTS, D_MODEL, FFN_CH), F32),
            pltpu.VMEM((FFN_STAGE_SLOTS, FFN_CH, D_MODEL), F32),
            pltpu.SemaphoreType.DMA((3, FFN_STAGE_SLOTS)),
        ],
        compiler_params=pltpu.CompilerParams(
            dimension_semantics=("arbitrary",), vmem_limit_bytes=VMEM_LIMIT),
        name="ffn_final" if final else "ffn",
    )(x2, g, wg, wu, wd, fg)


def kernel(x, norm_mix, w_in, att_sinks, ret_gn_gain, w_att_o, w_ret_o, w_out, norm_ffn, w_gate, w_up, w_down,
           final_norm):
    batch, seq, d = x.shape
    depth = w_in.shape[0]
    assert d == D_MODEL and seq % MIX_BLK == 0 and (batch * seq) % PROJ_BM == 0 and (batch * seq) % FFN_BM == 0
    x2 = x.reshape(batch * seq, d)
    consts = _mixer_constants()
    fg = final_norm.reshape(1, d)
    for l in range(depth):
        proj = _inproj(x2, norm_mix[l].reshape(1, d), w_in, l)
        sink_row = jnp.repeat(att_sinks[l].astype(F32).reshape(ATT_KV_HEADS, ATT_GROUP), CHUNK, axis=1)[:, None, :]
        x2 = _mixer(proj, x2, consts, sink_row, ret_gn_gain[l].reshape(1, RET_V_W),
                    w_att_o[l].astype(BF16), w_ret_o[l].astype(BF16), w_out[l].astype(BF16), batch, seq)
        x2 = _ffn(x2, norm_ffn[l].reshape(1, d), w_gate, w_up, w_down, fg, l, final=(l == depth - 1))
    return x2.reshape(batch, seq, d)
```

```python
import functools

import jax
import jax.numpy as jnp
import numpy as np
from jax import lax
from jax.experimental import pallas as pl
from jax.experimental.pallas import tpu as pltpu

F32 = jnp.float32
BF16 = jnp.bfloat16

D_MODEL = 1024
ATT_HEADS = 8
ATT_KV_HEADS = 2
ATT_HEAD_DIM = 64
ATT_GROUP = ATT_HEADS // ATT_KV_HEADS
WINDOW = 128
CHUNK = 128
RET_HEADS = 4
RET_KEY_DIM = 128
RET_VAL_DIM = 256
D_FF = 2816
EPS = 1e-6

ATT_Q_W = ATT_HEADS * ATT_HEAD_DIM
ATT_KV_W = ATT_KV_HEADS * ATT_HEAD_DIM
RET_QK_W = RET_HEADS * RET_KEY_DIM
RET_V_W = RET_HEADS * RET_VAL_DIM
OFF_AQ = 0
OFF_AK = OFF_AQ + ATT_Q_W
OFF_AV = OFF_AK + ATT_KV_W
OFF_RQ = OFF_AV + ATT_KV_W
OFF_RK = OFF_RQ + RET_QK_W
OFF_RV = OFF_RK + RET_QK_W
OFF_RG = OFF_RV + RET_V_W
OFF_GA = OFF_RG + RET_V_W
OFF_GR = OFF_GA + D_MODEL
D_IN = OFF_GR + D_MODEL
MAIN_W = OFF_GA
GATE_W = D_IN - OFF_GA

NEG = -1e30
SUM_ROWS = 8

V7X_VMEM_BYTES = 64 * 1024 * 1024
VMEM_LIMIT = V7X_VMEM_BYTES - 8 * 1024 * 1024

PROJ_BM = 512
PROJ_NCH = 256
MIX_BLK = 512
MERGE_BM = 1024
FFN_BM = 1024
FFN_CH = 256


def _rms(x, g):
    return x * lax.rsqrt(jnp.mean(x * x, axis=-1, keepdims=True) + EPS) * g


def _dot(a, b):
    return jnp.dot(a, b, preferred_element_type=F32)


def _dot_nt(a, b):
    return lax.dot_general(a, b, (((1,), (1,)), ((), ())), preferred_element_type=F32)


def _dot_tn(a, b):
    return lax.dot_general(a, b, (((0,), (0,)), ((), ())), preferred_element_type=F32)


def _inproj_kernel(x_ref, g_ref, w_ref, main_ref, gate_ref):
    h = _rms(x_ref[...], g_ref[...]).astype(BF16)
    for c in range(D_IN // PROJ_NCH):
        lo = c * PROJ_NCH
        y = _dot(h, w_ref[:, lo:lo + PROJ_NCH].astype(BF16)).astype(BF16)
        if lo < MAIN_W:
            main_ref[:, lo:lo + PROJ_NCH] = y
        else:
            gate_ref[:, lo - MAIN_W:lo - MAIN_W + PROJ_NCH] = y


def _inproj(x2, g, w, layer):
    t = x2.shape[0]
    assert MAIN_W % PROJ_NCH == 0
    return pl.pallas_call(
        _inproj_kernel,
        grid=(t // PROJ_BM,),
        in_specs=[
            pl.BlockSpec((PROJ_BM, D_MODEL), lambda i: (i, 0)),
            pl.BlockSpec((1, D_MODEL), lambda i: (0, 0)),
            pl.BlockSpec((None, D_MODEL, D_IN), lambda i: (layer, 0, 0), pipeline_mode=pl.Buffered(1)),
        ],
        out_specs=[pl.BlockSpec((PROJ_BM, MAIN_W), lambda i: (i, 0)),
                   pl.BlockSpec((PROJ_BM, GATE_W), lambda i: (i, 0))],
        out_shape=[jax.ShapeDtypeStruct((t, MAIN_W), BF16), jax.ShapeDtypeStruct((t, GATE_W), BF16)],
        compiler_params=pltpu.CompilerParams(
            dimension_semantics=("arbitrary",), vmem_limit_bytes=VMEM_LIMIT),
        name="inproj",
    )(x2, g, w)


def _mixer_kernel(p_ref, kvp_ref, bias_ref, sink_ref, dintra_ref, dq_ref, dk_ref, dc_ref,
                  gain_ref, att_t, ret_s, state):
    j = pl.program_id(1)

    @pl.when(j == 0)
    def _():
        state[...] = jnp.zeros_like(state)

    first = jnp.where(j == 0, 1, 0)

    for c in range(MIX_BLK // CHUNK):
        rows = slice(c * CHUNK, (c + 1) * CHUNK)
        if c == 0:
            kv_p = kvp_ref[...]
        else:
            kv_p = p_ref[(c - 1) * CHUNK:c * CHUNK, OFF_AK:OFF_AK + 2 * ATT_KV_W]
        kv_c = p_ref[rows, OFF_AK:OFF_AK + 2 * ATT_KV_W]
        kv = jnp.concatenate([kv_p, kv_c], axis=0)
        q = p_ref[rows, OFF_AQ:OFF_AQ + ATT_Q_W]
        for kh in range(ATT_KV_HEADS):
            qs = jnp.concatenate(
                [q[:, (kh * ATT_GROUP + g) * ATT_HEAD_DIM:(kh * ATT_GROUP + g + 1) * ATT_HEAD_DIM]
                 for g in range(ATT_GROUP)], axis=0)
            kk = kv[:, kh * ATT_HEAD_DIM:(kh + 1) * ATT_HEAD_DIM] * (ATT_HEAD_DIM ** -0.5)
            vv = kv[:, ATT_KV_W + kh * ATT_HEAD_DIM:ATT_KV_W + (kh + 1) * ATT_HEAD_DIM]
            bias = bias_ref[first * ATT_KV_HEADS + kh] if c == 0 else bias_ref[kh]
            s = _dot_nt(kk, qs) + bias
            sink = sink_ref[kh]
            m = jnp.maximum(jnp.max(s, axis=0, keepdims=True), sink)
            p = jnp.exp((s - m).astype(BF16))
            ov = _dot_tn(jnp.concatenate([vv, jnp.ones((2 * CHUNK, SUM_ROWS), BF16)], axis=1), p)
            denom = ov[ATT_HEAD_DIM:ATT_HEAD_DIM + 1] + jnp.exp(sink - m)
            o = ov[:ATT_HEAD_DIM] / denom
            for g in range(ATT_GROUP):
                h = kh * ATT_GROUP + g
                att_t[h * ATT_HEAD_DIM:(h + 1) * ATT_HEAD_DIM, rows] = o[:, g * CHUNK:(g + 1) * CHUNK].astype(BF16)

        for h in range(RET_HEADS):
            rq = p_ref[rows, OFF_RQ + h * RET_KEY_DIM:OFF_RQ + (h + 1) * RET_KEY_DIM]
            rk = p_ref[rows, OFF_RK + h * RET_KEY_DIM:OFF_RK + (h + 1) * RET_KEY_DIM]
            rv = p_ref[rows, OFF_RV + h * RET_VAL_DIM:OFF_RV + (h + 1) * RET_VAL_DIM]
            st = state[h]
            sc = (_dot_nt(rq, rk) * dintra_ref[h]).astype(BF16)
            qd = (rq.astype(F32) * dq_ref[h]).astype(BF16)
            kd = (rk.astype(F32) * dk_ref[h]).astype(BF16)
            lhs = jnp.concatenate([jnp.concatenate([sc, qd], axis=1),
                                   jnp.concatenate([kd.T, jnp.zeros_like(kd)], axis=1)], axis=0)
            res = _dot(lhs, jnp.concatenate([rv, st.astype(BF16)], axis=0))
            o = res[:CHUNK]
            state[h] = dc_ref[h] * st + res[CHUNK:]
            mu = jnp.mean(o, axis=-1, keepdims=True)
            oc = o - mu
            var = jnp.mean(oc * oc, axis=-1, keepdims=True)
            vcols = slice(h * RET_VAL_DIM, (h + 1) * RET_VAL_DIM)
            y = oc * lax.rsqrt(var + EPS) * gain_ref[:, vcols]
            rg = p_ref[rows, OFF_RG + h * RET_VAL_DIM:OFF_RG + (h + 1) * RET_VAL_DIM].astype(F32)
            ret_s[rows, vcols] = (y * (rg * jax.nn.sigmoid(rg))).astype(BF16)


def _merge_kernel(att_ref, ret_ref, gate_ref, x_ref, wa_ref, wr_ref, wo_ref, o_ref):
    a = _dot_tn(att_ref[...], wa_ref[...])
    r = _dot(ret_ref[...], wr_ref[...])
    ga = gate_ref[:, :D_MODEL].astype(F32)
    gr = gate_ref[:, D_MODEL:].astype(F32)
    merged = (jax.nn.sigmoid(ga) * a + jax.nn.sigmoid(gr) * r).astype(BF16)
    o_ref[...] = x_ref[...] + _dot(merged, wo_ref[...])


def _merge(att_t, ret, gates, x2, wa, wr, wo):
    t = x2.shape[0]
    row = lambda i: (i, 0)
    const = lambda i: (0, 0)
    resident = functools.partial(pl.BlockSpec, pipeline_mode=pl.Buffered(1))
    return pl.pallas_call(
        _merge_kernel,
        grid=(t // MERGE_BM,),
        in_specs=[
            pl.BlockSpec((ATT_Q_W, MERGE_BM), lambda i: (0, i)),
            pl.BlockSpec((MERGE_BM, RET_V_W), row),
            pl.BlockSpec((MERGE_BM, GATE_W), row),
            pl.BlockSpec((MERGE_BM, D_MODEL), row),
            resident(wa.shape, const),
            resident(wr.shape, const),
            resident(wo.shape, const),
        ],
        out_specs=pl.BlockSpec((MERGE_BM, D_MODEL), row),
        out_shape=jax.ShapeDtypeStruct(x2.shape, F32),
        compiler_params=pltpu.CompilerParams(
            dimension_semantics=("arbitrary",), vmem_limit_bytes=VMEM_LIMIT),
        name="merge",
    )(att_t, ret, gates, x2, wa, wr, wo)


def _mixer(proj, consts, sink_row, gain, batch, seq):
    bias, dintra, dq, dk, dc = consts
    t = proj.shape[0]
    nblk = seq // MIX_BLK
    cpb = MIX_BLK // CHUNK
    kv_w = 2 * ATT_KV_W
    assert OFF_AK % kv_w == 0
    row = lambda b, j: (b * nblk + j, 0)
    halo = lambda b, j: (jnp.maximum((b * nblk + j) * cpb - 1, 0), OFF_AK // kv_w)
    const2 = lambda b, j: (0, 0)
    const3 = lambda b, j: (0, 0, 0)
    resident = functools.partial(pl.BlockSpec, pipeline_mode=pl.Buffered(1))
    return pl.pallas_call(
        _mixer_kernel,
        grid=(batch, nblk),
        in_specs=[
            pl.BlockSpec((MIX_BLK, MAIN_W), row),
            pl.BlockSpec((CHUNK, kv_w), halo),
            resident(bias.shape, const3),
            resident(sink_row.shape, const3),
            resident(dintra.shape, const3),
            resident(dq.shape, const3),
            resident(dk.shape, const3),
            pl.BlockSpec(memory_space=pltpu.SMEM),
            resident(gain.shape, const2),
        ],
        out_specs=[pl.BlockSpec((ATT_Q_W, MIX_BLK), lambda b, j: (0, b * nblk + j)),
                   pl.BlockSpec((MIX_BLK, RET_V_W), row)],
        out_shape=[jax.ShapeDtypeStruct((ATT_Q_W, t), BF16),
                   jax.ShapeDtypeStruct((t, RET_V_W), BF16)],
        scratch_shapes=[
            pltpu.VMEM((RET_HEADS, RET_KEY_DIM, RET_VAL_DIM), F32),
        ],
        compiler_params=pltpu.CompilerParams(
            dimension_semantics=("arbitrary", "arbitrary"), vmem_limit_bytes=VMEM_LIMIT),
        name="mixer",
    )(proj, proj, bias, sink_row, dintra, dq, dk, dc, gain)


def _mixer_constants():
    f32 = np.float32
    i = np.arange(CHUNK)[None, :]
    jj = np.arange(2 * CHUNK)[:, None]
    dist = i + CHUNK - jj
    valid = (dist >= 0) & (dist < WINDOW)
    slopes = np.exp2(f32(-8.0) * (np.arange(ATT_HEADS, dtype=f32) + f32(1.0)) / f32(ATT_HEADS))

    def table(ok):
        b = np.where(ok[None], -slopes[:, None, None] * dist.astype(f32)[None], f32(NEG))
        b = b.reshape(ATT_KV_HEADS, ATT_GROUP, 2 * CHUNK, CHUNK).transpose(0, 2, 1, 3)
        return b.reshape(ATT_KV_HEADS, 2 * CHUNK, ATT_GROUP * CHUNK)

    bias = np.concatenate([table(valid), table(valid & (jj >= CHUNK))], axis=0)

    log_g = np.log(f32(1.0) - np.exp2(f32(-5.0) - np.arange(RET_HEADS, dtype=f32)))
    idx = np.arange(CHUNK, dtype=f32)
    diff = idx[:, None] - idx[None, :]
    scale = f32(RET_KEY_DIM ** -0.5)
    dintra = np.where(diff >= 0, np.exp(log_g[:, None, None] * np.maximum(diff, f32(0.0))), f32(0.0)) * scale
    dq = np.exp(log_g[:, None] * (idx + f32(1.0)))
    dk = np.exp(log_g[:, None] * (f32(CHUNK - 1.0) - idx)) * scale
    dq = np.broadcast_to(dq[:, :, None], (RET_HEADS, CHUNK, RET_KEY_DIM))
    dk = np.broadcast_to(dk[:, :, None], (RET_HEADS, CHUNK, RET_KEY_DIM))
    dc = np.exp(log_g * f32(CHUNK))
    return tuple(jnp.asarray(np.ascontiguousarray(t), dtype=F32) for t in (bias, dintra, dq, dk, dc))


def _ffn_kernel(x_ref, g_ref, wg_ref, wu_ref, wd_ref, fg_ref, o_ref, *, final):
    x = x_ref[...]
    h = _rms(x, g_ref[...]).astype(BF16)
    acc = x
    for c in range(D_FF // FFN_CH):
        cols = slice(c * FFN_CH, (c + 1) * FFN_CH)
        gate = _dot(h, wg_ref[:, cols].astype(BF16))
        up = _dot(h, wu_ref[:, cols].astype(BF16))
        act = (gate * jax.nn.sigmoid(gate) * up).astype(BF16)
        acc = acc + _dot(act, wd_ref[cols, :].astype(BF16))
    if final:
        acc = _rms(acc, fg_ref[...])
    o_ref[...] = acc


def _ffn(x2, g, wg, wu, wd, fg, layer, final):
    t = x2.shape[0]
    const = lambda i: (0, 0)

    def resident(shape, _):
        return pl.BlockSpec((None,) + tuple(shape[1:]), lambda i: (layer, 0, 0), pipeline_mode=pl.Buffered(1))

    return pl.pallas_call(
        functools.partial(_ffn_kernel, final=final),
        grid=(t // FFN_BM,),
        in_specs=[
            pl.BlockSpec((FFN_BM, D_MODEL), lambda i: (i, 0)),
            pl.BlockSpec((1, D_MODEL), const),
            resident(wg.shape, const),
            resident(wu.shape, const),
            resident(wd.shape, const),
            pl.BlockSpec((1, D_MODEL), const),
        ],
        out_specs=pl.BlockSpec((FFN_BM, D_MODEL), lambda i: (i, 0)),
        out_shape=jax.ShapeDtypeStruct(x2.shape, F32),
        compiler_params=pltpu.CompilerParams(
            dimension_semantics=("arbitrary",), vmem_limit_bytes=VMEM_LIMIT),
        name="ffn_final" if final else "ffn",
    )(x2, g, wg, wu, wd, fg)


def kernel(x, norm_mix, w_in, att_sinks, ret_gn_gain, w_att_o, w_ret_o, w_out, norm_ffn, w_gate, w_up, w_down,
           final_norm):
    batch, seq, d = x.shape
    depth = w_in.shape[0]
    assert d == D_MODEL and seq % MIX_BLK == 0 and (batch * seq) % PROJ_BM == 0 and (batch * seq) % FFN_BM == 0
    x2 = x.reshape(batch * seq, d)
    consts = _mixer_constants()
    fg = final_norm.reshape(1, d)
    for l in range(depth):
        proj, gates = _inproj(x2, norm_mix[l].reshape(1, d), w_in, l)
        sink_row = jnp.repeat(att_sinks[l].astype(F32).reshape(ATT_KV_HEADS, ATT_GROUP), CHUNK, axis=1)[:, None, :]
        att_t, ret = _mixer(proj, consts, sink_row, ret_gn_gain[l].reshape(1, RET_V_W), batch, seq)
        x2 = _merge(att_t, ret, gates, x2,
                    w_att_o[l].astype(BF16), w_ret_o[l].astype(BF16), w_out[l].astype(BF16))
        x2 = _ffn(x2, norm_ffn[l].reshape(1, d), w_gate, w_up, w_down, fg, l, final=(l == depth - 1))
    return x2.reshape(batch, seq, d)
```

```python
import functools

import jax
import jax.numpy as jnp
import numpy as np
from jax import lax
from jax.experimental import pallas as pl
from jax.experimental.pallas import tpu as pltpu

F32 = jnp.float32
BF16 = jnp.bfloat16

D_MODEL = 1024
ATT_HEADS = 8
ATT_KV_HEADS = 2
ATT_HEAD_DIM = 64
ATT_GROUP = ATT_HEADS // ATT_KV_HEADS
WINDOW = 128
CHUNK = 128
RET_HEADS = 4
RET_KEY_DIM = 128
RET_VAL_DIM = 256
RET_CHUNK = 256
D_FF = 2816
EPS = 1e-6

ATT_Q_W = ATT_HEADS * ATT_HEAD_DIM
ATT_KV_W = ATT_KV_HEADS * ATT_HEAD_DIM
RET_QK_W = RET_HEADS * RET_KEY_DIM
RET_V_W = RET_HEADS * RET_VAL_DIM
OFF_AQ = 0
OFF_AK = OFF_AQ + ATT_Q_W
OFF_AV = OFF_AK + ATT_KV_W
OFF_RQ = OFF_AV + ATT_KV_W
OFF_RK = OFF_RQ + RET_QK_W
OFF_RV = OFF_RK + RET_QK_W
OFF_RG = OFF_RV + RET_V_W
OFF_GA = OFF_RG + RET_V_W
OFF_GR = OFF_GA + D_MODEL
D_IN = OFF_GR + D_MODEL

NEG = -1e30
SUM_ROWS = 8

V7X_VMEM_BYTES = 64 * 1024 * 1024
VMEM_LIMIT = V7X_VMEM_BYTES - 8 * 1024 * 1024

PROJ_BM = 512
PROJ_NCH = 256
MIX_BLK = 512
FFN_BM = 1024
FFN_CH = 256


def _rms(x, g):
    return x * lax.rsqrt(jnp.mean(x * x, axis=-1, keepdims=True) + EPS) * g


def _dot(a, b):
    return jnp.dot(a, b, preferred_element_type=F32)


def _dot_nt(a, b):
    return lax.dot_general(a, b, (((1,), (1,)), ((), ())), preferred_element_type=F32)


def _dot_tn(a, b):
    return lax.dot_general(a, b, (((0,), (0,)), ((), ())), preferred_element_type=F32)


def _inproj_kernel(x_ref, g_ref, w_ref, o_ref):
    h = _rms(x_ref[...], g_ref[...]).astype(BF16)
    for c in range(D_IN // PROJ_NCH):
        cols = slice(c * PROJ_NCH, (c + 1) * PROJ_NCH)
        o_ref[:, cols] = _dot(h, w_ref[:, cols].astype(BF16)).astype(BF16)


def _inproj(x2, g, w, layer):
    t = x2.shape[0]
    return pl.pallas_call(
        _inproj_kernel,
        grid=(t // PROJ_BM,),
        in_specs=[
            pl.BlockSpec((PROJ_BM, D_MODEL), lambda i: (i, 0)),
            pl.BlockSpec((1, D_MODEL), lambda i: (0, 0)),
            pl.BlockSpec((None, D_MODEL, D_IN), lambda i: (layer, 0, 0), pipeline_mode=pl.Buffered(1)),
        ],
        out_specs=pl.BlockSpec((PROJ_BM, D_IN), lambda i: (i, 0)),
        out_shape=jax.ShapeDtypeStruct((t, D_IN), BF16),
        compiler_params=pltpu.CompilerParams(
            dimension_semantics=("arbitrary",), vmem_limit_bytes=VMEM_LIMIT),
        name="inproj",
    )(x2, g, w)


def _mixer_kernel(p_ref, kvp_ref, x_ref, bias_ref, sink_ref, dintra_ref, dq_ref, dk_ref, dc_ref,
                  gain_ref, wa_ref, wr_ref, wo_ref, o_ref,
                  state, att_t, ret_s):
    j = pl.program_id(1)

    @pl.when(j == 0)
    def _():
        state[...] = jnp.zeros_like(state)

    first = jnp.where(j == 0, 1, 0)

    for c in range(MIX_BLK // CHUNK):
        rows = slice(c * CHUNK, (c + 1) * CHUNK)
        if c == 0:
            kv_p = kvp_ref[...]
        else:
            kv_p = p_ref[(c - 1) * CHUNK:c * CHUNK, OFF_AK:OFF_AK + 2 * ATT_KV_W]
        kv_c = p_ref[rows, OFF_AK:OFF_AK + 2 * ATT_KV_W]
        kv = jnp.concatenate([kv_p, kv_c], axis=0)
        q = p_ref[rows, OFF_AQ:OFF_AQ + ATT_Q_W]
        for kh in range(ATT_KV_HEADS):
            qs = jnp.concatenate(
                [q[:, (kh * ATT_GROUP + g) * ATT_HEAD_DIM:(kh * ATT_GROUP + g + 1) * ATT_HEAD_DIM]
                 for g in range(ATT_GROUP)], axis=0)
            kk = kv[:, kh * ATT_HEAD_DIM:(kh + 1) * ATT_HEAD_DIM] * (ATT_HEAD_DIM ** -0.5)
            vv = kv[:, ATT_KV_W + kh * ATT_HEAD_DIM:ATT_KV_W + (kh + 1) * ATT_HEAD_DIM]
            bias = bias_ref[first * ATT_KV_HEADS + kh] if c == 0 else bias_ref[kh]
            s = _dot_nt(kk, qs) + bias
            sink = sink_ref[kh]
            m = jnp.maximum(jnp.max(s, axis=0, keepdims=True), sink)
            p = jnp.exp((s - m).astype(BF16))
            ov = _dot_tn(jnp.concatenate([vv, jnp.ones((2 * CHUNK, SUM_ROWS), BF16)], axis=1), p)
            denom = ov[ATT_HEAD_DIM:ATT_HEAD_DIM + 1] + jnp.exp(sink - m)
            o = ov[:ATT_HEAD_DIM] / denom
            for g in range(ATT_GROUP):
                h = kh * ATT_GROUP + g
                att_t[h * ATT_HEAD_DIM:(h + 1) * ATT_HEAD_DIM, rows] = o[:, g * CHUNK:(g + 1) * CHUNK].astype(BF16)

    for rc in range(MIX_BLK // RET_CHUNK):
        rows = slice(rc * RET_CHUNK, (rc + 1) * RET_CHUNK)
        for h in range(RET_HEADS):
            rq = p_ref[rows, OFF_RQ + h * RET_KEY_DIM:OFF_RQ + (h + 1) * RET_KEY_DIM]
            rk = p_ref[rows, OFF_RK + h * RET_KEY_DIM:OFF_RK + (h + 1) * RET_KEY_DIM]
            rv = p_ref[rows, OFF_RV + h * RET_VAL_DIM:OFF_RV + (h + 1) * RET_VAL_DIM]
            st = state[h]
            sc = (_dot_nt(rq, rk) * dintra_ref[h]).astype(BF16)
            qd = (rq.astype(F32) * dq_ref[h]).astype(BF16)
            kd = (rk.astype(F32) * dk_ref[h]).astype(BF16)
            lhs = jnp.concatenate([jnp.concatenate([sc, qd], axis=1),
                                   jnp.concatenate([kd.T, jnp.zeros((RET_KEY_DIM, RET_KEY_DIM), BF16)], axis=1)],
                                  axis=0)
            res = _dot(lhs, jnp.concatenate([rv, st.astype(BF16)], axis=0))
            o = res[:RET_CHUNK]
            state[h] = dc_ref[h] * st + res[RET_CHUNK:]
            mu = jnp.mean(o, axis=-1, keepdims=True)
            oc = o - mu
            var = jnp.mean(oc * oc, axis=-1, keepdims=True)
            vcols = slice(h * RET_VAL_DIM, (h + 1) * RET_VAL_DIM)
            y = oc * lax.rsqrt(var + EPS) * gain_ref[:, vcols]
            rg = p_ref[rows, OFF_RG + h * RET_VAL_DIM:OFF_RG + (h + 1) * RET_VAL_DIM].astype(F32)
            ret_s[rows, vcols] = (y * (rg * jax.nn.sigmoid(rg))).astype(BF16)

    a = _dot_tn(att_t[...], wa_ref[...])
    r = _dot(ret_s[...], wr_ref[...])
    ga = p_ref[:, OFF_GA:OFF_GA + D_MODEL].astype(F32)
    gr = p_ref[:, OFF_GR:OFF_GR + D_MODEL].astype(F32)
    merged = (jax.nn.sigmoid(ga) * a + jax.nn.sigmoid(gr) * r).astype(BF16)
    o_ref[...] = x_ref[...] + _dot(merged, wo_ref[...])


def _mixer(proj, x2, consts, sink_row, gain, wa, wr, wo, batch, seq):
    bias, dintra, dq, dk, dc = consts
    nblk = seq // MIX_BLK
    cpb = MIX_BLK // CHUNK
    kv_w = 2 * ATT_KV_W
    assert OFF_AK % kv_w == 0
    row = lambda b, j: (b * nblk + j, 0)
    halo = lambda b, j: (jnp.maximum((b * nblk + j) * cpb - 1, 0), OFF_AK // kv_w)
    const2 = lambda b, j: (0, 0)
    const3 = lambda b, j: (0, 0, 0)
    resident = functools.partial(pl.BlockSpec, pipeline_mode=pl.Buffered(1))
    return pl.pallas_call(
        _mixer_kernel,
        grid=(batch, nblk),
        in_specs=[
            pl.BlockSpec((MIX_BLK, D_IN), row),
            pl.BlockSpec((CHUNK, kv_w), halo),
            pl.BlockSpec((MIX_BLK, D_MODEL), row),
            resident(bias.shape, const3),
            resident(sink_row.shape, const3),
            resident(dintra.shape, const3),
            resident(dq.shape, const3),
            resident(dk.shape, const3),
            pl.BlockSpec(memory_space=pltpu.SMEM),
            resident(gain.shape, const2),
            resident(wa.shape, const2),
            resident(wr.shape, const2),
            resident(wo.shape, const2),
        ],
        out_specs=pl.BlockSpec((MIX_BLK, D_MODEL), row),
        out_shape=jax.ShapeDtypeStruct(x2.shape, F32),
        scratch_shapes=[
            pltpu.VMEM((RET_HEADS, RET_KEY_DIM, RET_VAL_DIM), F32),
            pltpu.VMEM((ATT_Q_W, MIX_BLK), BF16),
            pltpu.VMEM((MIX_BLK, RET_V_W), BF16),
        ],
        compiler_params=pltpu.CompilerParams(
            dimension_semantics=("arbitrary", "arbitrary"), vmem_limit_bytes=VMEM_LIMIT),
        name="mixer",
    )(proj, proj, x2, bias, sink_row, dintra, dq, dk, dc, gain, wa, wr, wo)


def _mixer_constants():
    f32 = np.float32
    i = np.arange(CHUNK)[None, :]
    jj = np.arange(2 * CHUNK)[:, None]
    dist = i + CHUNK - jj
    valid = (dist >= 0) & (dist < WINDOW)
    slopes = np.exp2(f32(-8.0) * (np.arange(ATT_HEADS, dtype=f32) + f32(1.0)) / f32(ATT_HEADS))

    def table(ok):
        b = np.where(ok[None], -slopes[:, None, None] * dist.astype(f32)[None], f32(NEG))
        b = b.reshape(ATT_KV_HEADS, ATT_GROUP, 2 * CHUNK, CHUNK).transpose(0, 2, 1, 3)
        return b.reshape(ATT_KV_HEADS, 2 * CHUNK, ATT_GROUP * CHUNK)

    bias = np.concatenate([table(valid), table(valid & (jj >= CHUNK))], axis=0)

    log_g = np.log(f32(1.0) - np.exp2(f32(-5.0) - np.arange(RET_HEADS, dtype=f32)))
    idx = np.arange(RET_CHUNK, dtype=f32)
    diff = idx[:, None] - idx[None, :]
    scale = f32(RET_KEY_DIM ** -0.5)
    dintra = np.where(diff >= 0, np.exp(log_g[:, None, None] * np.maximum(diff, f32(0.0))), f32(0.0)) * scale
    dq = np.exp(log_g[:, None] * (idx + f32(1.0)))
    dk = np.exp(log_g[:, None] * (f32(RET_CHUNK - 1.0) - idx)) * scale
    dq = np.broadcast_to(dq[:, :, None], (RET_HEADS, RET_CHUNK, RET_KEY_DIM))
    dk = np.broadcast_to(dk[:, :, None], (RET_HEADS, RET_CHUNK, RET_KEY_DIM))
    dc = np.exp(log_g * f32(RET_CHUNK))
    return tuple(jnp.asarray(np.ascontiguousarray(t), dtype=F32) for t in (bias, dintra, dq, dk, dc))


def _ffn_kernel(x_ref, g_ref, wg_ref, wu_ref, wd_ref, fg_ref, o_ref, *, final):
    x = x_ref[...]
    h = _rms(x, g_ref[...]).astype(BF16)
    acc = x
    for c in range(D_FF // FFN_CH):
        cols = slice(c * FFN_CH, (c + 1) * FFN_CH)
        gate = _dot(h, wg_ref[:, cols].astype(BF16))
        up = _dot(h, wu_ref[:, cols].astype(BF16))
        act = (gate * jax.nn.sigmoid(gate) * up).astype(BF16)
        acc = acc + _dot(act, wd_ref[cols, :].astype(BF16))
    if final:
        acc = _rms(acc, fg_ref[...])
    o_ref[...] = acc


def _ffn(x2, g, wg, wu, wd, fg, layer, final):
    t = x2.shape[0]
    const = lambda i: (0, 0)

    def resident(shape, _):
        return pl.BlockSpec((None,) + tuple(shape[1:]), lambda i: (layer, 0, 0), pipeline_mode=pl.Buffered(1))

    return pl.pallas_call(
        functools.partial(_ffn_kernel, final=final),
        grid=(t // FFN_BM,),
        in_specs=[
            pl.BlockSpec((FFN_BM, D_MODEL), lambda i: (i, 0)),
            pl.BlockSpec((1, D_MODEL), const),
            resident(wg.shape, const),
            resident(wu.shape, const),
            resident(wd.shape, const),
            pl.BlockSpec((1, D_MODEL), const),
        ],
        out_specs=pl.BlockSpec((FFN_BM, D_MODEL), lambda i: (i, 0)),
        out_shape=jax.ShapeDtypeStruct(x2.shape, F32),
        compiler_params=pltpu.CompilerParams(
            dimension_semantics=("arbitrary",), vmem_limit_bytes=VMEM_LIMIT),
        name="ffn_final" if final else "ffn",
    )(x2, g, wg, wu, wd, fg)


def kernel(x, norm_mix, w_in, att_sinks, ret_gn_gain, w_att_o, w_ret_o, w_out, norm_ffn, w_gate, w_up, w_down,
           final_norm):
    batch, seq, d = x.shape
    depth = w_in.shape[0]
    assert d == D_MODEL and seq % MIX_BLK == 0 and (batch * seq) % PROJ_BM == 0 and (batch * seq) % FFN_BM == 0
    assert MIX_BLK % RET_CHUNK == 0 and MIX_BLK % CHUNK == 0
    x2 = x.reshape(batch * seq, d)
    consts = _mixer_constants()
    fg = final_norm.reshape(1, d)
    for l in range(depth):
        proj = _inproj(x2, norm_mix[l].reshape(1, d), w_in, l)
        sink_row = jnp.repeat(att_sinks[l].astype(F32).reshape(ATT_KV_HEADS, ATT_GROUP), CHUNK, axis=1)[:, None, :]
        x2 = _mixer(proj, x2, consts, sink_row, ret_gn_gain[l].reshape(1, RET_V_W),
                    w_att_o[l].astype(BF16), w_ret_o[l].astype(BF16), w_out[l].astype(BF16), batch, seq)
        x2 = _ffn(x2, norm_ffn[l].reshape(1, d), w_gate, w_up, w_down, fg, l, final=(l == depth - 1))
    return x2.reshape(batch, seq, d)
```

```python
import functools

import jax
import jax.numpy as jnp
import numpy as np
from jax import lax
from jax.experimental import pallas as pl
from jax.experimental.pallas import tpu as pltpu

F32 = jnp.float32
BF16 = jnp.bfloat16

D_MODEL = 1024
ATT_HEADS = 8
ATT_KV_HEADS = 2
ATT_HEAD_DIM = 64
ATT_GROUP = ATT_HEADS // ATT_KV_HEADS
WINDOW = 128
CHUNK = 128
RET_HEADS = 4
RET_KEY_DIM = 128
RET_VAL_DIM = 256
RET_CHUNK = 512
D_FF = 2816
EPS = 1e-6

ATT_Q_W = ATT_HEADS * ATT_HEAD_DIM
ATT_KV_W = ATT_KV_HEADS * ATT_HEAD_DIM
RET_QK_W = RET_HEADS * RET_KEY_DIM
RET_V_W = RET_HEADS * RET_VAL_DIM
OFF_AQ = 0
OFF_AK = OFF_AQ + ATT_Q_W
OFF_AV = OFF_AK + ATT_KV_W
OFF_RQ = OFF_AV + ATT_KV_W
OFF_RK = OFF_RQ + RET_QK_W
OFF_RV = OFF_RK + RET_QK_W
OFF_RG = OFF_RV + RET_V_W
OFF_GA = OFF_RG + RET_V_W
OFF_GR = OFF_GA + D_MODEL
D_IN = OFF_GR + D_MODEL

NEG = -1e30
SUM_ROWS = 8

V7X_VMEM_BYTES = 64 * 1024 * 1024
VMEM_LIMIT = V7X_VMEM_BYTES - 8 * 1024 * 1024

PROJ_BM = 512
PROJ_NCH = 256
MIX_BLK = 512
FFN_BM = 1024
FFN_CH = 256


def _rms(x, g):
    return x * lax.rsqrt(jnp.mean(x * x, axis=-1, keepdims=True) + EPS) * g


def _dot(a, b):
    return jnp.dot(a, b, preferred_element_type=F32)


def _dot_nt(a, b):
    return lax.dot_general(a, b, (((1,), (1,)), ((), ())), preferred_element_type=F32)


def _dot_tn(a, b):
    return lax.dot_general(a, b, (((0,), (0,)), ((), ())), preferred_element_type=F32)


def _inproj_kernel(x_ref, g_ref, w_ref, o_ref):
    h = _rms(x_ref[...], g_ref[...]).astype(BF16)
    for c in range(D_IN // PROJ_NCH):
        cols = slice(c * PROJ_NCH, (c + 1) * PROJ_NCH)
        o_ref[:, cols] = _dot(h, w_ref[:, cols].astype(BF16)).astype(BF16)


def _inproj(x2, g, w, layer):
    t = x2.shape[0]
    return pl.pallas_call(
        _inproj_kernel,
        grid=(t // PROJ_BM,),
        in_specs=[
            pl.BlockSpec((PROJ_BM, D_MODEL), lambda i: (i, 0)),
            pl.BlockSpec((1, D_MODEL), lambda i: (0, 0)),
            pl.BlockSpec((None, D_MODEL, D_IN), lambda i: (layer, 0, 0), pipeline_mode=pl.Buffered(1)),
        ],
        out_specs=pl.BlockSpec((PROJ_BM, D_IN), lambda i: (i, 0)),
        out_shape=jax.ShapeDtypeStruct((t, D_IN), BF16),
        compiler_params=pltpu.CompilerParams(
            dimension_semantics=("arbitrary",), vmem_limit_bytes=VMEM_LIMIT),
        name="inproj",
    )(x2, g, w)


def _mixer_kernel(p_ref, kvp_ref, x_ref, bias_ref, sink_ref, dintra_ref, dq_ref, dk_ref, dc_ref,
                  gain_ref, wa_ref, wr_ref, wo_ref, o_ref,
                  state, att_t, ret_s):
    j = pl.program_id(1)

    @pl.when(j == 0)
    def _():
        state[...] = jnp.zeros_like(state)

    first = jnp.where(j == 0, 1, 0)

    for c in range(MIX_BLK // CHUNK):
        rows = slice(c * CHUNK, (c + 1) * CHUNK)
        if c == 0:
            kv_p = kvp_ref[...]
        else:
            kv_p = p_ref[(c - 1) * CHUNK:c * CHUNK, OFF_AK:OFF_AK + 2 * ATT_KV_W]
        kv_c = p_ref[rows, OFF_AK:OFF_AK + 2 * ATT_KV_W]
        kv = jnp.concatenate([kv_p, kv_c], axis=0)
        q = p_ref[rows, OFF_AQ:OFF_AQ + ATT_Q_W]
        for kh in range(ATT_KV_HEADS):
            qs = jnp.concatenate(
                [q[:, (kh * ATT_GROUP + g) * ATT_HEAD_DIM:(kh * ATT_GROUP + g + 1) * ATT_HEAD_DIM]
                 for g in range(ATT_GROUP)], axis=0)
            kk = kv[:, kh * ATT_HEAD_DIM:(kh + 1) * ATT_HEAD_DIM] * (ATT_HEAD_DIM ** -0.5)
            vv = kv[:, ATT_KV_W + kh * ATT_HEAD_DIM:ATT_KV_W + (kh + 1) * ATT_HEAD_DIM]
            bias = bias_ref[first * ATT_KV_HEADS + kh] if c == 0 else bias_ref[kh]
            s = _dot_nt(kk, qs) + bias
            sink = sink_ref[kh]
            m = jnp.maximum(jnp.max(s, axis=0, keepdims=True), sink)
            p = jnp.exp((s - m).astype(BF16))
            ov = _dot_tn(jnp.concatenate([vv, jnp.ones((2 * CHUNK, SUM_ROWS), BF16)], axis=1), p)
            denom = ov[ATT_HEAD_DIM:ATT_HEAD_DIM + 1] + jnp.exp(sink - m)
            o = ov[:ATT_HEAD_DIM] / denom
            for g in range(ATT_GROUP):
                h = kh * ATT_GROUP + g
                att_t[h * ATT_HEAD_DIM:(h + 1) * ATT_HEAD_DIM, rows] = o[:, g * CHUNK:(g + 1) * CHUNK].astype(BF16)

    for rc in range(MIX_BLK // RET_CHUNK):
        rows = slice(rc * RET_CHUNK, (rc + 1) * RET_CHUNK)
        for h in range(RET_HEADS):
            rq = p_ref[rows, OFF_RQ + h * RET_KEY_DIM:OFF_RQ + (h + 1) * RET_KEY_DIM]
            rk = p_ref[rows, OFF_RK + h * RET_KEY_DIM:OFF_RK + (h + 1) * RET_KEY_DIM]
            rv = p_ref[rows, OFF_RV + h * RET_VAL_DIM:OFF_RV + (h + 1) * RET_VAL_DIM]
            st = state[h]
            sc = (_dot_nt(rq, rk) * dintra_ref[h]).astype(BF16)
            qd = (rq.astype(F32) * dq_ref[h]).astype(BF16)
            kd = (rk.astype(F32) * dk_ref[h]).astype(BF16)
            lhs = jnp.concatenate([jnp.concatenate([sc, qd], axis=1),
                                   jnp.concatenate([kd.T, jnp.zeros((RET_KEY_DIM, RET_KEY_DIM), BF16)], axis=1)],
                                  axis=0)
            res = _dot(lhs, jnp.concatenate([rv, st.astype(BF16)], axis=0))
            o = res[:RET_CHUNK]
            state[h] = dc_ref[h] * st + res[RET_CHUNK:]
            mu = jnp.mean(o, axis=-1, keepdims=True)
            oc = o - mu
            var = jnp.mean(oc * oc, axis=-1, keepdims=True)
            vcols = slice(h * RET_VAL_DIM, (h + 1) * RET_VAL_DIM)
            y = oc * lax.rsqrt(var + EPS) * gain_ref[:, vcols]
            rg = p_ref[rows, OFF_RG + h * RET_VAL_DIM:OFF_RG + (h + 1) * RET_VAL_DIM].astype(F32)
            ret_s[rows, vcols] = (y * (rg * jax.nn.sigmoid(rg))).astype(BF16)

    a = _dot_tn(att_t[...], wa_ref[...])
    r = _dot(ret_s[...], wr_ref[...])
    ga = p_ref[:, OFF_GA:OFF_GA + D_MODEL].astype(F32)
    gr = p_ref[:, OFF_GR:OFF_GR + D_MODEL].astype(F32)
    merged = (jax.nn.sigmoid(ga) * a + jax.nn.sigmoid(gr) * r).astype(BF16)
    o_ref[...] = x_ref[...] + _dot(merged, wo_ref[...])


def _mixer(proj, x2, consts, sink_row, gain, wa, wr, wo, batch, seq):
    bias, dintra, dq, dk, dc = consts
    nblk = seq // MIX_BLK
    cpb = MIX_BLK // CHUNK
    kv_w = 2 * ATT_KV_W
    assert OFF_AK % kv_w == 0
    row = lambda b, j: (b * nblk + j, 0)
    halo = lambda b, j: (jnp.maximum((b * nblk + j) * cpb - 1, 0), OFF_AK // kv_w)
    const2 = lambda b, j: (0, 0)
    const3 = lambda b, j: (0, 0, 0)
    resident = functools.partial(pl.BlockSpec, pipeline_mode=pl.Buffered(1))
    return pl.pallas_call(
        _mixer_kernel,
        grid=(batch, nblk),
        in_specs=[
            pl.BlockSpec((MIX_BLK, D_IN), row),
            pl.BlockSpec((CHUNK, kv_w), halo),
            pl.BlockSpec((MIX_BLK, D_MODEL), row),
            resident(bias.shape, const3),
            resident(sink_row.shape, const3),
            resident(dintra.shape, const3),
            resident(dq.shape, const3),
            resident(dk.shape, const3),
            pl.BlockSpec(memory_space=pltpu.SMEM),
            resident(gain.shape, const2),
            resident(wa.shape, const2),
            resident(wr.shape, const2),
            resident(wo.shape, const2),
        ],
        out_specs=pl.BlockSpec((MIX_BLK, D_MODEL), row),
        out_shape=jax.ShapeDtypeStruct(x2.shape, F32),
        scratch_shapes=[
            pltpu.VMEM((RET_HEADS, RET_KEY_DIM, RET_VAL_DIM), F32),
            pltpu.VMEM((ATT_Q_W, MIX_BLK), BF16),
            pltpu.VMEM((MIX_BLK, RET_V_W), BF16),
        ],
        compiler_params=pltpu.CompilerParams(
            dimension_semantics=("arbitrary", "arbitrary"), vmem_limit_bytes=VMEM_LIMIT),
        name="mixer",
    )(proj, proj, x2, bias, sink_row, dintra, dq, dk, dc, gain, wa, wr, wo)


def _mixer_constants():
    f32 = np.float32
    i = np.arange(CHUNK)[None, :]
    jj = np.arange(2 * CHUNK)[:, None]
    dist = i + CHUNK - jj
    valid = (dist >= 0) & (dist < WINDOW)
    slopes = np.exp2(f32(-8.0) * (np.arange(ATT_HEADS, dtype=f32) + f32(1.0)) / f32(ATT_HEADS))

    def table(ok):
        b = np.where(ok[None], -slopes[:, None, None] * dist.astype(f32)[None], f32(NEG))
        b = b.reshape(ATT_KV_HEADS, ATT_GROUP, 2 * CHUNK, CHUNK).transpose(0, 2, 1, 3)
        return b.reshape(ATT_KV_HEADS, 2 * CHUNK, ATT_GROUP * CHUNK)

    bias = np.concatenate([table(valid), table(valid & (jj >= CHUNK))], axis=0)

    log_g = np.log(f32(1.0) - np.exp2(f32(-5.0) - np.arange(RET_HEADS, dtype=f32)))
    idx = np.arange(RET_CHUNK, dtype=f32)
    diff = idx[:, None] - idx[None, :]
    scale = f32(RET_KEY_DIM ** -0.5)
    dintra = np.where(diff >= 0, np.exp(log_g[:, None, None] * np.maximum(diff, f32(0.0))), f32(0.0)) * scale
    dq = np.exp(log_g[:, None] * (idx + f32(1.0)))
    dk = np.exp(log_g[:, None] * (f32(RET_CHUNK - 1.0) - idx)) * scale
    dq = np.broadcast_to(dq[:, :, None], (RET_HEADS, RET_CHUNK, RET_KEY_DIM))
    dk = np.broadcast_to(dk[:, :, None], (RET_HEADS, RET_CHUNK, RET_KEY_DIM))
    dc = np.exp(log_g * f32(RET_CHUNK))
    return tuple(jnp.asarray(np.ascontiguousarray(t), dtype=F32) for t in (bias, dintra, dq, dk, dc))


def _ffn_kernel(x_ref, g_ref, wg_ref, wu_ref, wd_ref, fg_ref, o_ref, *, final):
    x = x_ref[...]
    h = _rms(x, g_ref[...]).astype(BF16)
    acc = x
    for c in range(D_FF // FFN_CH):
        cols = slice(c * FFN_CH, (c + 1) * FFN_CH)
        gate = _dot(h, wg_ref[:, cols].astype(BF16))
        up = _dot(h, wu_ref[:, cols].astype(BF16))
        act = (gate * jax.nn.sigmoid(gate) * up).astype(BF16)
        acc = acc + _dot(act, wd_ref[cols, :].astype(BF16))
    if final:
        acc = _rms(acc, fg_ref[...])
    o_ref[...] = acc


def _ffn(x2, g, wg, wu, wd, fg, layer, final):
    t = x2.shape[0]
    const = lambda i: (0, 0)

    def resident(shape, _):
        return pl.BlockSpec((None,) + tuple(shape[1:]), lambda i: (layer, 0, 0), pipeline_mode=pl.Buffered(1))

    return pl.pallas_call(
        functools.partial(_ffn_kernel, final=final),
        grid=(t // FFN_BM,),
        in_specs=[
            pl.BlockSpec((FFN_BM, D_MODEL), lambda i: (i, 0)),
            pl.BlockSpec((1, D_MODEL), const),
            resident(wg.shape, const),
            resident(wu.shape, const),
            resident(wd.shape, const),
            pl.BlockSpec((1, D_MODEL), const),
        ],
        out_specs=pl.BlockSpec((FFN_BM, D_MODEL), lambda i: (i, 0)),
        out_shape=jax.ShapeDtypeStruct(x2.shape, F32),
        compiler_params=pltpu.CompilerParams(
            dimension_semantics=("arbitrary",), vmem_limit_bytes=VMEM_LIMIT),
        name="ffn_final" if final else "ffn",
    )(x2, g, wg, wu, wd, fg)


def kernel(x, norm_mix, w_in, att_sinks, ret_gn_gain, w_att_o, w_ret_o, w_out, norm_ffn, w_gate, w_up, w_down,
           final_norm):
    batch, seq, d = x.shape
    depth = w_in.shape[0]
    assert d == D_MODEL and seq % MIX_BLK == 0 and (batch * seq) % PROJ_BM == 0 and (batch * seq) % FFN_BM == 0
    assert MIX_BLK % RET_CHUNK == 0 and MIX_BLK % CHUNK == 0
    x2 = x.reshape(batch * seq, d)
    consts = _mixer_constants()
    fg = final_norm.reshape(1, d)
    for l in range(depth):
        proj = _inproj(x2, norm_mix[l].reshape(1, d), w_in, l)
        sink_row = jnp.repeat(att_sinks[l].astype(F32).reshape(ATT_KV_HEADS, ATT_GROUP), CHUNK, axis=1)[:, None, :]
        x2 = _mixer(proj, x2, consts, sink_row, ret_gn_gain[l].reshape(1, RET_V_W),
                    w_att_o[l].astype(BF16), w_ret_o[l].astype(BF16), w_out[l].astype(BF16), batch, seq)
        x2 = _ffn(x2, norm_ffn[l].reshape(1, d), w_gate, w_up, w_down, fg, l, final=(l == depth - 1))
    return x2.reshape(batch, seq, d)
```
